```python
import math
import jax, jax.numpy as jnp
from jax import lax
import numpy as np

D_MODEL = 4096
BATCH = 2
SEQ = 4096
DEPTH = 1
DEC_BATCH = 128
DEC_SEQ = 8
PAST_LEN = 2048
PAGE_SIZE = 128

ATT_HD = 128
ATT_W = D_MODEL // 2
ATT_HEADS = ATT_W // ATT_HD
MOBA_BLOCK = 256
MOBA_TOPK = 3
Q_CHUNK = 32
ATT_SCALE = ATT_HD ** -0.5
N_BUCKETS = 32
MAX_DISTANCE = 1024
RW_HD = 64
RW_W = D_MODEL // 2
RW_HEADS = RW_W // RW_HD
W_LORA = max(32, int(round(RW_W ** 0.5 * 1.8 / 32)) * 32)
A_LORA = max(32, int(round(RW_W ** 0.5 * 1.8 / 32)) * 32)
G_LORA = max(32, int(round(RW_W ** 0.8 * 0.6 / 32)) * 32)
RW_COLS = 3 * RW_W + W_LORA + A_LORA + G_LORA
LNX_EPS = 64e-5
IN_COLS = 3 * ATT_W + RW_COLS + 2 * D_MODEL
N_EXPERTS = 32
TOP_K = 4
D_FF = D_MODEL
SWIGLU_LIMIT = 7.0
SWIGLU_ALPHA = 1.702
MOE_BLOCK = 128
RMS_EPS = 1e-6
POOL_NUM = 5
POOL_DEN = 4

kernel_name = 'moba_rwkv7_gated_hybrid_moe_step'


def rms_norm(x, g):
    xf = x.astype(jnp.float32)
    y = xf * lax.rsqrt(jnp.mean(xf * xf, axis=-1, keepdims=True) + RMS_EPS)
    return (y * g.astype(jnp.float32)).astype(x.dtype)


def t5_bucket(dist):
    n = jnp.maximum(dist, 0)
    max_exact = N_BUCKETS // 2
    nf = jnp.maximum(n, 1).astype(jnp.float32)
    large = max_exact + (jnp.log(nf / max_exact) / math.log(MAX_DISTANCE / max_exact)
                         * (N_BUCKETS - max_exact)).astype(jnp.int32)
    return jnp.where(n < max_exact, n, jnp.minimum(large, N_BUCKETS - 1))


def to_blocks(rows, n_blocks):
    pad = n_blocks * MOBA_BLOCK - rows.shape[0]
    rows = jnp.pad(rows, ((0, pad), (0, 0), (0, 0)))
    return rows.reshape(n_blocks, MOBA_BLOCK, rows.shape[1], rows.shape[2]).transpose(2, 0, 1, 3)


def moba_core(q, kb, vb, kmean, q_pos, rel_bias):
    n_heads, t_q = q.shape[0], q.shape[1]
    n_blocks = kb.shape[1]
    n_sel = min(MOBA_TOPK, n_blocks)
    qf = q.astype(jnp.float32)
    cur = q_pos // MOBA_BLOCK
    score = jnp.einsum('htd,hnd->htn', qf, kmean)
    is_past = jnp.arange(n_blocks, dtype=jnp.int32)[None, :] < cur[:, None]
    score = jnp.where(is_past[None], score, -jnp.inf)
    top_s, top_i = lax.top_k(score, n_sel)
    blk = jnp.concatenate([top_i.astype(jnp.int32),
                           jnp.broadcast_to(cur[None, :, None], (n_heads, t_q, 1))], axis=-1)
    keep = jnp.concatenate([jnp.isfinite(top_s), jnp.ones((n_heads, t_q, 1), bool)], axis=-1)
    kg = jax.vmap(lambda kh, bh: kh[bh])(kb, blk)
    vg = jax.vmap(lambda vh, bh: vh[bh])(vb, blk)
    k_pos = blk[..., None] * MOBA_BLOCK + jnp.arange(MOBA_BLOCK, dtype=jnp.int32)
    dist = q_pos[None, :, None, None] - k_pos
    bias = rel_bias.T[jnp.arange(n_heads)[:, None, None, None], t5_bucket(dist)].astype(jnp.float32)
    logits = jnp.einsum('htd,htjsd->htjs', qf, kg.astype(jnp.float32)) * ATT_SCALE + bias
    visible = keep[..., None] & (dist >= 0)
    logits = jnp.where(visible, logits, -jnp.inf).reshape(n_heads, t_q, -1)
    p = jax.nn.softmax(logits, axis=-1).reshape(kg.shape[:4])
    return jnp.einsum('htjs,htjsd->htd', p.astype(vg.dtype), vg)


def moba_prompt(q, k, v, rel_bias):
    n_seq, t_len, n_heads, hd = q.shape
    n_blocks = -(-t_len // MOBA_BLOCK)
    kb = jax.vmap(lambda r: to_blocks(r, n_blocks))(k)
    vb = jax.vmap(lambda r: to_blocks(r, n_blocks))(v)
    kmean = jnp.mean(kb.astype(jnp.float32), axis=3)
    n_chunks = t_len // Q_CHUNK
    qc = q.reshape(n_seq, n_chunks, Q_CHUNK, n_heads, hd).transpose(0, 1, 3, 2, 4)
    qc = qc.reshape(n_seq * n_chunks, n_heads, Q_CHUNK, hd)
    seq_idx = jnp.repeat(jnp.arange(n_seq, dtype=jnp.int32), n_chunks)
    start = jnp.tile(jnp.arange(n_chunks, dtype=jnp.int32) * Q_CHUNK, n_seq)

    def chunk(args):
        qi, n, s0 = args
        pos = s0 + jnp.arange(Q_CHUNK, dtype=jnp.int32)
        return moba_core(qi, kb[n], vb[n], kmean[n], pos, rel_bias)

    out = lax.map(chunk, (qc, seq_idx, start))
    out = out.reshape(n_seq, n_chunks, n_heads, Q_CHUNK, hd).transpose(0, 1, 3, 2, 4)
    return out.reshape(n_seq, t_len, n_heads, hd)


def moba_sample(q, k_new, v_new, cache_k, cache_v, page_table, rel_bias):
    n_seq, t_new, n_heads, hd = q.shape
    past_len = page_table.shape[1] * cache_k.shape[1]
    n_blocks = -(-(past_len + t_new) // MOBA_BLOCK)
    q_pos = past_len + jnp.arange(t_new, dtype=jnp.int32)

    def one_seq(args):
        qn, kn, vn, pages = args
        k_all = jnp.concatenate([cache_k[pages].reshape(past_len, n_heads, hd), kn], axis=0)
        v_all = jnp.concatenate([cache_v[pages].reshape(past_len, n_heads, hd), vn], axis=0)
        kb = to_blocks(k_all, n_blocks)
        vb = to_blocks(v_all, n_blocks)
        kmean = jnp.mean(kb.astype(jnp.float32), axis=2)
        return moba_core(qn.transpose(1, 0, 2), kb, vb, kmean, q_pos, rel_bias)

    out = lax.map(one_seq, (q, k_new, v_new, page_table))
    return out.transpose(0, 2, 1, 3)


def rwkv7_scan(s0, r, decay, k, v, a_vec, b_vec):
    def step(s, inp):
        r_t, w_t, k_t, v_t, a_t, b_t = inp
        sa = jnp.einsum('nhvk,nhk->nhv', s, a_t)
        s = s * w_t[:, :, None, :] + sa[..., None] * b_t[:, :, None, :] + v_t[..., None] * k_t[:, :, None, :]
        return s, jnp.einsum('nhvk,nhk->nhv', s, r_t)

    xs = (r.swapaxes(0, 1), decay.swapaxes(0, 1), k.swapaxes(0, 1),
          v.swapaxes(0, 1), a_vec.swapaxes(0, 1), b_vec.swapaxes(0, 1))
    s, y = lax.scan(step, s0.astype(jnp.float32), xs)
    return s, y.swapaxes(0, 1)


def rwkv7_mix(rw, shift0, s0, mu_shift, w0, w_lora_up, a0, a_lora_up, g_lora_up,
              k_k, k_a, r_k, lnx_w, lnx_b):
    f32 = jnp.float32
    n_seq, t_len, _ = rw.shape
    prev = jnp.concatenate([shift0[:, None, :].astype(rw.dtype), rw[:, :-1]], axis=1)
    m = rw + (prev - rw) * mu_shift
    c = 3 * RW_W
    r, k, v = m[..., :RW_W], m[..., RW_W:2 * RW_W], m[..., 2 * RW_W:c]
    xw = m[..., c:c + W_LORA]
    xa = m[..., c + W_LORA:c + W_LORA + A_LORA]
    xg = m[..., c + W_LORA + A_LORA:]
    w_log = -jax.nn.softplus(-(w0 + jnp.tanh(xw) @ w_lora_up).astype(f32)) - 0.5
    decay = jnp.exp(-jnp.exp(w_log))
    a = jax.nn.sigmoid((a0 + xa @ a_lora_up).astype(f32))
    g = jax.nn.sigmoid(xg) @ g_lora_up
    hs = lambda t: t.astype(f32).reshape(n_seq, t_len, RW_HEADS, RW_HD)
    r, k, v, decay, a = hs(r), hs(k), hs(v), hs(decay), hs(a)
    kk = k * k_k.reshape(RW_HEADS, RW_HD)
    kk = kk / jnp.maximum(jnp.sqrt(jnp.sum(kk * kk, axis=-1, keepdims=True)), 1e-12)
    k = k * (1 + (a - 1) * k_a.reshape(RW_HEADS, RW_HD))
    s, y = rwkv7_scan(s0, r, decay, k, v, -kk, kk * a)
    mu = jnp.mean(y, axis=-1, keepdims=True)
    var = jnp.mean(jnp.square(y - mu), axis=-1, keepdims=True)
    yn = ((y - mu) * lax.rsqrt(var + LNX_EPS)).reshape(n_seq, t_len, RW_W) * lnx_w + lnx_b
    bonus = (jnp.sum(r * k * r_k, axis=-1, keepdims=True) * v).reshape(n_seq, t_len, RW_W)
    out = ((yn + bonus) * g).astype(rw.dtype)
    return out, s, rw[:, -1]


def moe_ffn(h, router_w, router_b, w_gu, b_gu, w_down, b_down):
    n_tok, d = h.shape
    logits = (h @ router_w).astype(jnp.float32) + router_b.astype(jnp.float32)
    top_v, top_e = lax.top_k(logits, TOP_K)
    gate = jax.nn.softmax(top_v, axis=-1)
    n_assign = n_tok * TOP_K
    e_flat = top_e.reshape(n_assign).astype(jnp.int32)
    tok_flat = jnp.arange(n_assign, dtype=jnp.int32) // TOP_K
    order = jnp.argsort(e_flat)
    e_sorted = e_flat[order]
    counts = jnp.zeros((N_EXPERTS,), jnp.int32).at[e_flat].add(1)
    padded = (counts + MOE_BLOCK - 1) // MOE_BLOCK * MOE_BLOCK
    pad_end = jnp.cumsum(padded)
    pad_start = pad_end - padded
    start = jnp.cumsum(counts) - counts
    dest = pad_start[e_sorted] + jnp.arange(n_assign, dtype=jnp.int32) - start[e_sorted]
    n_blocks = (n_assign + N_EXPERTS * (MOE_BLOCK - 1) + MOE_BLOCK - 1) // MOE_BLOCK
    n_rows = n_blocks * MOE_BLOCK
    row_tok = jnp.full((n_rows,), n_tok, jnp.int32).at[dest].set(tok_flat[order])
    row_gate = jnp.zeros((n_rows,), jnp.float32).at[dest].set(gate.reshape(n_assign)[order])
    block_e = jnp.minimum(jnp.searchsorted(pad_end, jnp.arange(n_blocks, dtype=jnp.int32) * MOE_BLOCK,
                                           side='right'), N_EXPERTS - 1).astype(jnp.int32)
    xb = jnp.concatenate([h, jnp.zeros((1, d), h.dtype)], axis=0)[row_tok].reshape(n_blocks, MOE_BLOCK, d)

    def expert_block(args):
        xe, e = args
        gu = xe @ w_gu[e] + b_gu[e]
        g_ = jnp.minimum(gu[:, :D_FF], SWIGLU_LIMIT)
        u_ = jnp.clip(gu[:, D_FF:], -SWIGLU_LIMIT, SWIGLU_LIMIT)
        act = (u_ + 1) * (g_ * jax.nn.sigmoid(SWIGLU_ALPHA * g_))
        return act @ w_down[e] + b_down[e]

    yb = lax.map(expert_block, (xb, block_e)).reshape(n_rows, d)
    y = jax.ops.segment_sum(yb * row_gate[:, None].astype(yb.dtype), row_tok, num_segments=n_tok + 1)
    return y[:n_tok].astype(h.dtype)


def hybrid_layer(x, attend, shift0, s0, norm1_g, w_in, q_norm_g, k_norm_g, rw_params,
                 w_branch_a, w_branch_b, w_out, norm2_g, moe_params):
    n_seq, t_len, _ = x.shape
    h = rms_norm(x, norm1_g)
    p = h @ w_in
    q = rms_norm(p[..., :ATT_W].reshape(n_seq, t_len, ATT_HEADS, ATT_HD), q_norm_g)
    k = rms_norm(p[..., ATT_W:2 * ATT_W].reshape(n_seq, t_len, ATT_HEADS, ATT_HD), k_norm_g)
    v = p[..., 2 * ATT_W:3 * ATT_W].reshape(n_seq, t_len, ATT_HEADS, ATT_HD)
    rw = p[..., 3 * ATT_W:3 * ATT_W + RW_COLS]
    gates = jax.nn.sigmoid(p[..., 3 * ATT_W + RW_COLS:])
    o_att = attend(q, k, v).reshape(n_seq, t_len, ATT_W)
    o_rw, s_new, last = rwkv7_mix(rw, shift0, s0, *rw_params)
    merged = gates[..., :D_MODEL] * (o_att @ w_branch_a) + gates[..., D_MODEL:] * (o_rw @ w_branch_b)
    x = x + merged @ w_out
    ffn = moe_ffn(rms_norm(x, norm2_g).reshape(n_seq * t_len, D_MODEL), *moe_params)
    return x + ffn.reshape(x.shape), k, v, s_new, last


def setup_inputs(seed: int = 0) -> dict:
    key = jax.random.key(seed)
    kit = iter(jax.random.split(key, 40))
    nrm = lambda shape, s: jax.random.normal(next(kit), shape, jnp.float32) * s
    n_pages = PAST_LEN // PAGE_SIZE
    n_used = DEC_BATCH * n_pages
    n_pool = -(-n_used * POOL_NUM // POOL_DEN)
    page_table = jax.random.permutation(next(kit), n_pool)[:n_used].reshape(DEC_BATCH, n_pages).astype(jnp.int32)
    return {
        'x_prompt': nrm((BATCH, SEQ, D_MODEL), 1.0),
        'x_sample': nrm((DEC_BATCH, DEC_SEQ, D_MODEL), 1.0),
        'cache_k': nrm((n_pool, PAGE_SIZE, ATT_HEADS, ATT_HD), 1.0),
        'cache_v': nrm((n_pool, PAGE_SIZE, ATT_HEADS, ATT_HD), 1.0),
        'page_table': page_table,
        'state_rwkv': nrm((DEC_BATCH, RW_HEADS, RW_HD, RW_HD), 0.5),
        'state_shift': nrm((DEC_BATCH, RW_COLS), 1.0),
        'norm1_g': 1.0 + nrm((D_MODEL,), 0.02),
        'w_in': nrm((D_MODEL, IN_COLS), D_MODEL ** -0.5),
        'q_norm_g': 1.0 + nrm((ATT_HD,), 0.02),
        'k_norm_g': 1.0 + nrm((ATT_HD,), 0.02),
        'rel_bias': nrm((N_BUCKETS, ATT_HEADS), 0.1),
        'mu_shift': jax.random.uniform(next(kit), (RW_COLS,), jnp.float32),
        'w0': nrm((RW_W,), 0.5),
        'w_lora_up': nrm((W_LORA, RW_W), 0.5 * W_LORA ** -0.5),
        'a0': nrm((RW_W,), 0.1),
        'a_lora_up': nrm((A_LORA, RW_W), 0.5 * A_LORA ** -0.5),
        'g_lora_up': nrm((G_LORA, RW_W), G_LORA ** -0.5),
        'k_k': 0.85 + nrm((RW_W,), 0.02),
        'k_a': 1.0 + nrm((RW_W,), 0.02),
        'r_k': nrm((RW_HEADS, RW_HD), 0.1),
        'lnx_w': 1.0 + nrm((RW_W,), 0.02),
        'lnx_b': nrm((RW_W,), 0.01),
        'w_branch_a': nrm((ATT_W, D_MODEL), ATT_W ** -0.5),
        'w_branch_b': nrm((RW_W, D_MODEL), RW_W ** -0.5),
        'w_out': nrm((D_MODEL, D_MODEL), D_MODEL ** -0.5),
        'norm2_g': 1.0 + nrm((D_MODEL,), 0.02),
        'router_w': nrm((D_MODEL, N_EXPERTS), D_MODEL ** -0.5),
        'router_b': nrm((N_EXPERTS,), 0.01),
        'w_gu': nrm((N_EXPERTS, D_MODEL, 2 * D_FF), D_MODEL ** -0.5),
        'b_gu': nrm((N_EXPERTS, 2 * D_FF), 0.01),
        'w_down': nrm((N_EXPERTS, D_FF, D_MODEL), D_FF ** -0.5),
        'b_down': nrm((N_EXPERTS, D_MODEL), 0.01),
    }


def reference(x_prompt, x_sample, cache_k, cache_v, page_table, state_rwkv, state_shift,
              norm1_g, w_in, q_norm_g, k_norm_g, rel_bias, mu_shift, w0, w_lora_up, a0,
              a_lora_up, g_lora_up, k_k, k_a, r_k, lnx_w, lnx_b, w_branch_a, w_branch_b,
              w_out, norm2_g, router_w, router_b, w_gu, b_gu, w_down, b_down):
    rw_params = (mu_shift, w0, w_lora_up, a0, a_lora_up, g_lora_up, k_k, k_a, r_k, lnx_w, lnx_b)
    moe_params = (router_w, router_b, w_gu, b_gu, w_down, b_down)
    n_prompt = x_prompt.shape[0]
    shift_zero = jnp.zeros((n_prompt, RW_COLS), x_prompt.dtype)
    state_zero = jnp.zeros((n_prompt, RW_HEADS, RW_HD, RW_HD), jnp.float32)
    y_prompt, k_prompt, v_prompt, rwkv_prompt, shift_prompt = hybrid_layer(
        x_prompt, lambda q, k, v: moba_prompt(q, k, v, rel_bias), shift_zero, state_zero,
        norm1_g, w_in, q_norm_g, k_norm_g, rw_params, w_branch_a, w_branch_b, w_out, norm2_g, moe_params)
    y_sample, k_sample, v_sample, rwkv_sample, shift_sample = hybrid_layer(
        x_sample, lambda q, k, v: moba_sample(q, k, v, cache_k, cache_v, page_table, rel_bias),
        state_shift, state_rwkv,
        norm1_g, w_in, q_norm_g, k_norm_g, rw_params, w_branch_a, w_branch_b, w_out, norm2_g, moe_params)
    return (y_prompt, y_sample, k_prompt, v_prompt, k_sample, v_sample,
            rwkv_prompt, shift_prompt, rwkv_sample, shift_sample)
```

```python
import functools
import math

import jax
import jax.numpy as jnp
from jax import lax
from jax.experimental import pallas as pl
from jax.experimental.pallas import tpu as pltpu

F32 = jnp.float32
BF16 = jnp.bfloat16
I32 = jnp.int32

LANES = 128
SUBLANES = 8
VMEM_CAP_BYTES = 60 * 1024 * 1024
VMEM_SLACK_BYTES = 6 * 1024 * 1024

D_MODEL = 4096
ATT_HD = 128
ATT_W = D_MODEL // 2
ATT_HEADS = ATT_W // ATT_HD
MOBA_BLOCK = 256
MOBA_TOPK = 3
ATT_SCALE = ATT_HD ** -0.5
N_BUCKETS = 32
MAX_DISTANCE = 1024
RW_HD = 64
RW_W = D_MODEL // 2
RW_HEADS = RW_W // RW_HD
W_LORA = max(32, int(round(RW_W ** 0.5 * 1.8 / 32)) * 32)
A_LORA = max(32, int(round(RW_W ** 0.5 * 1.8 / 32)) * 32)
G_LORA = max(32, int(round(RW_W ** 0.8 * 0.6 / 32)) * 32)
RW_COLS = 3 * RW_W + W_LORA + A_LORA + G_LORA
LORA_COLS = W_LORA + A_LORA + G_LORA
LNX_EPS = 64e-5
N_EXPERTS = 32
TOP_K = 4
D_FF = D_MODEL
SWIGLU_LIMIT = 7.0
SWIGLU_ALPHA = 1.702
RMS_EPS = 1e-6
NEG = -1e30
BIG_INDEX = 1e9

RW_LANE_TILES = RW_W // LANES
MM_TM = 512
MM_TN = 512
RW_PAD = -(-RW_COLS // MM_TN) * MM_TN
LORA_PAD = RW_PAD - 3 * RW_W
MOE_TM = 256
MOE_TN_GU = 256
MOE_TN_DOWN = 512


def _vmem_limit(*buffer_bytes):
    return int(min(VMEM_CAP_BYTES, sum(buffer_bytes) + VMEM_SLACK_BYTES))


def _nbytes(shape, dtype):
    return math.prod(shape) * jnp.dtype(dtype).itemsize


def _split3(x):
    x1 = x.astype(BF16)
    r1 = x - x1.astype(F32)
    x2 = r1.astype(BF16)
    x3 = (r1 - x2.astype(F32)).astype(BF16)
    return x1, x2, x3


def _dot(a, b):
    return jnp.dot(a, b, preferred_element_type=F32)


def _rmsnorm_kernel(x_ref, g_ref, o_ref):
    x = x_ref[...].astype(F32)
    y = x * lax.rsqrt(jnp.mean(x * x, axis=-1, keepdims=True) + RMS_EPS)
    o_ref[...] = (y * g_ref[...]).astype(o_ref.dtype)


def rmsnorm_rows(x, g, out_dtype, tb):
    m, d = x.shape
    return pl.pallas_call(
        _rmsnorm_kernel,
        out_shape=jax.ShapeDtypeStruct((m, d), out_dtype),
        grid=(m // tb,),
        in_specs=[pl.BlockSpec((tb, d), lambda i: (i, 0)),
                  pl.BlockSpec((1, d), lambda i: (0, 0))],
        out_specs=pl.BlockSpec((tb, d), lambda i: (i, 0)),
        compiler_params=pltpu.CompilerParams(
            dimension_semantics=("arbitrary",),
            vmem_limit_bytes=_vmem_limit(4 * _nbytes((tb, d), F32))),
        name="rmsnorm_rows",
    )(x, g.reshape(1, d).astype(F32))


def _matmul_kernel(*refs, epilogue, has_addend):
    a_ref, w_ref = refs[0], refs[1]
    o_ref, wb_ref = refs[-2], refs[-1]

    @pl.when(pl.program_id(1) == 0)
    def _():
        wb_ref[...] = w_ref[...].astype(BF16)

    acc = _dot(a_ref[...], wb_ref[...])
    if epilogue == "gate":
        acc = jax.nn.sigmoid(refs[2][...]) * acc
        if has_addend:
            acc = acc + refs[3][...]
    elif epilogue == "residual":
        acc = refs[2][...] + acc
    o_ref[...] = acc.astype(o_ref.dtype)


def matmul(a, w, n_out, *, col_block0=0, epilogue=None, gate=None, addend=None,
           out_dtype=F32, name="matmul"):
    m, k = a.shape
    tm, tn = MM_TM, MM_TN
    extra, extra_specs = [], []
    io_spec = pl.BlockSpec((tm, tn), lambda j, i: (i, j))
    if epilogue == "gate":
        extra.append(gate)
        extra_specs.append(io_spec)
        if addend is not None:
            extra.append(addend)
            extra_specs.append(io_spec)
    elif epilogue == "residual":
        extra.append(addend)
        extra_specs.append(io_spec)
    kern = functools.partial(_matmul_kernel, epilogue=epilogue,
                             has_addend=(epilogue == "gate" and addend is not None))
    return pl.pallas_call(
        kern,
        out_shape=jax.ShapeDtypeStruct((m, n_out), out_dtype),
        grid=(n_out // tn, m // tm),
        in_specs=[pl.BlockSpec((tm, k), lambda j, i: (i, 0)),
                  pl.BlockSpec((k, tn), lambda j, i: (0, j + col_block0))] + extra_specs,
        out_specs=io_spec,
        scratch_shapes=[pltpu.VMEM((k, tn), BF16)],
        compiler_params=pltpu.CompilerParams(
            dimension_semantics=("arbitrary", "arbitrary"),
            vmem_limit_bytes=_vmem_limit(
                2 * _nbytes((tm, k), BF16), 2 * _nbytes((k, tn), F32), _nbytes((k, tn), BF16),
                2 * (2 + len(extra)) * _nbytes((tm, tn), F32))),
        name=name,
    )(a, w, *extra)


def _segsum(x, bd):
    outs = []
    for c in range(x.shape[-1] // LANES):
        x1, x2, x3 = _split3(x[:, c * LANES:(c + 1) * LANES])
        outs.append(_dot(x1, bd) + _dot(x2, bd) + _dot(x3, bd))
    return jnp.concatenate(outs, axis=-1)


def _softplus(z):
    return jnp.maximum(z, 0.0) + jnp.log(1.0 + jnp.exp(-jnp.abs(z)))


def _rwkv_prep_kernel(rw_ref, prev_ref, mu_ref, w0_ref, a0_ref, kk_g_ref, ka_ref, rk_ref,
                      wl_ref, al_ref, gl_ref, bd_ref,
                      w_o, r_o, a_o, b_o, kx_o, v_o, br_o, kr_o, g_o, bonus_o):
    rw = rw_ref[...]
    m = rw + (prev_ref[...] - rw) * mu_ref[...]
    r = m[:, 0:RW_W]
    k = m[:, RW_W:2 * RW_W]
    v = m[:, 2 * RW_W:3 * RW_W]
    x = m[:, 3 * RW_W:]
    bd = bd_ref[...]
    u = w0_ref[...] + _dot(jnp.tanh(x).astype(BF16), wl_ref[...])
    w_log = -_softplus(-u) - 0.5
    decay = jnp.exp(-jnp.exp(w_log))
    a = jax.nn.sigmoid(a0_ref[...] + _dot(x.astype(BF16), al_ref[...]))
    g = _dot(jax.nn.sigmoid(x).astype(BF16), gl_ref[...])
    kk = k * kk_g_ref[...]
    kk = kk / jnp.maximum(jnp.sqrt(_segsum(kk * kk, bd)), 1e-12)
    kx = k * (1.0 + (a - 1.0) * ka_ref[...])
    bvec = kk * a
    w_o[...] = decay
    r_o[...] = r
    a_o[...] = -kk
    b_o[...] = bvec
    kx_o[...] = kx
    v_o[...] = v
    br_o[...] = _segsum(bvec * r, bd)
    kr_o[...] = _segsum(kx * r, bd)
    g_o[...] = g
    bonus_o[...] = _segsum(r * kx * rk_ref[...], bd) * v


def rwkv_prep(rw, prev, vecs, loras, bd, tb):
    t = rw.shape[0]
    row = lambda width: pl.BlockSpec((tb, width), lambda i: (i, 0))
    const = lambda shape: pl.BlockSpec(shape, lambda i: (0, 0))
    mu, w0, a0, k_k, k_a, r_k = vecs
    n_out = 10
    return pl.pallas_call(
        _rwkv_prep_kernel,
        out_shape=[jax.ShapeDtypeStruct((t, RW_W), F32)] * n_out,
        grid=(t // tb,),
        in_specs=[row(RW_PAD), row(RW_PAD), const((1, RW_PAD))] + [const((1, RW_W))] * 5
                 + [const((LORA_PAD, RW_W))] * 3 + [const((LANES, LANES))],
        out_specs=[row(RW_W)] * n_out,
        compiler_params=pltpu.CompilerParams(
            dimension_semantics=("arbitrary",),
            vmem_limit_bytes=_vmem_limit(4 * _nbytes((tb, RW_PAD), F32),
                                         2 * n_out * _nbytes((tb, RW_W), F32),
                                         6 * _nbytes((LORA_PAD, RW_W), BF16),
                                         16 * _nbytes((tb, RW_W), F32))),
        name="rwkv_prep",
    )(rw, prev, mu, w0, a0, k_k, k_a, r_k, *loras, bd)


def _rwkv_scan_kernel(w_ref, r_ref, a_ref, b_ref, kx_ref, v_ref, br_ref, kr_ref, s0_ref, bd_ref,
                      ipat_ref, y_ref, s_ref, lhs_ref, res_ref, *, n_group, n_tok):
    @pl.when(pl.program_id(1) == 0)
    def _():
        s_ref[...] = s0_ref[...]

    ipat = ipat_ref[...]
    seg = 3 * RW_HD

    def step(t, carry):
        rows = []
        for g in range(n_group):
            w, r, a = w_ref[g, t], r_ref[g, t], a_ref[g, t]
            wr = w * r
            vv = v_ref[g, t]
            for c in range(RW_LANE_TILES):
                s = s_ref[g, c]
                base = (g * RW_LANE_TILES + c) * seg
                lhs_ref[base:base + RW_HD, :] = (s * a[c:c + 1]).astype(BF16)
                lhs_ref[base + RW_HD:base + 2 * RW_HD, :] = (s * wr[c:c + 1]).astype(BF16)
                lhs_ref[base + 2 * RW_HD:base + seg, :] = (ipat * vv[c:c + 1]).astype(BF16)
            rows.append((w, r))
        res_ref[...] = _dot(lhs_ref[...], bd_ref[...])
        for g in range(n_group):
            w = rows[g][0]
            b, kx, br, kr = b_ref[g, t], kx_ref[g, t], br_ref[g, t], kr_ref[g, t]
            for c in range(RW_LANE_TILES):
                base = (g * RW_LANE_TILES + c) * seg
                sa = res_ref[base:base + RW_HD, :]
                swr = res_ref[base + RW_HD:base + 2 * RW_HD, :]
                vx = res_ref[base + 2 * RW_HD:base + seg, :]
                s_ref[g, c] = s_ref[g, c] * w[c:c + 1] + sa * b[c:c + 1] + vx * kx[c:c + 1]
                y = swr + sa * br[c:c + 1] + vx * kr[c:c + 1]
                y_ref[g, t, pl.ds(c, 1), :] = jnp.sum(y * ipat, axis=0, keepdims=True)
        return carry

    lax.fori_loop(0, n_tok, step, 0)


def rwkv_scan(seq_inputs, s0, bd, ipat, n_group, n_tok):
    n, t = seq_inputs[0].shape[:2]
    tok = pl.BlockSpec((n_group, n_tok, RW_LANE_TILES, LANES), lambda i, j: (i, j, 0, 0))
    st = pl.BlockSpec((n_group, RW_LANE_TILES, RW_HD, LANES), lambda i, j: (i, 0, 0, 0))
    lhs_rows = n_group * RW_LANE_TILES * 3 * RW_HD
    kern = functools.partial(_rwkv_scan_kernel, n_group=n_group, n_tok=n_tok)
    tok_bytes = _nbytes((n_group, n_tok, RW_LANE_TILES, LANES), F32)
    st_bytes = _nbytes((n_group, RW_LANE_TILES, RW_HD, LANES), F32)
    return pl.pallas_call(
        kern,
        out_shape=[jax.ShapeDtypeStruct((n, t, RW_LANE_TILES, LANES), F32),
                   jax.ShapeDtypeStruct(s0.shape, F32)],
        grid=(n // n_group, t // n_tok),
        in_specs=[tok] * 8 + [st, pl.BlockSpec((LANES, LANES), lambda i, j: (0, 0)),
                              pl.BlockSpec((RW_HD, LANES), lambda i, j: (0, 0))],
        out_specs=[tok, st],
        scratch_shapes=[pltpu.VMEM((lhs_rows, LANES), BF16), pltpu.VMEM((lhs_rows, LANES), F32)],
        compiler_params=pltpu.CompilerParams(
            dimension_semantics=("arbitrary", "arbitrary"),
            vmem_limit_bytes=_vmem_limit(18 * tok_bytes, 4 * st_bytes,
                                         _nbytes((lhs_rows, LANES), BF16),
                                         _nbytes((lhs_rows, LANES), F32))),
        name="rwkv_scan",
    )(*seq_inputs, s0, bd, ipat)


def _rwkv_post_kernel(y_ref, g_ref, bonus_ref, lw_ref, lb_ref, bd_ref, o_ref):
    y = y_ref[...]
    bd = bd_ref[...]
    mu = _segsum(y, bd) * (1.0 / RW_HD)
    d = y - mu
    var = _segsum(d * d, bd) * (1.0 / RW_HD)
    yn = d * lax.rsqrt(var + LNX_EPS) * lw_ref[...] + lb_ref[...]
    o_ref[...] = ((yn + bonus_ref[...]) * g_ref[...]).astype(o_ref.dtype)


def rwkv_post(y, g, bonus, lnx_w, lnx_b, bd, tb):
    t = y.shape[0]
    row = pl.BlockSpec((tb, RW_W), lambda i: (i, 0))
    const = lambda shape: pl.BlockSpec(shape, lambda i: (0, 0))
    return pl.pallas_call(
        _rwkv_post_kernel,
        out_shape=jax.ShapeDtypeStruct((t, RW_W), BF16),
        grid=(t // tb,),
        in_specs=[row, row, row, const((1, RW_W)), const((1, RW_W)), const((LANES, LANES))],
        out_specs=row,
        compiler_params=pltpu.CompilerParams(
            dimension_semantics=("arbitrary",),
            vmem_limit_bytes=_vmem_limit(16 * _nbytes((tb, RW_W), F32))),
        name="rwkv_post",
    )(y, g, bonus, lnx_w, lnx_b, bd)


def _top_mask(score, index, n_pick, axis):
    index = index.astype(F32)
    sel = jnp.zeros(score.shape, F32)
    for _ in range(n_pick):
        mx = jnp.max(score, axis=axis, keepdims=True)
        first = jnp.min(jnp.where(score == mx, index, BIG_INDEX), axis=axis, keepdims=True)
        hit = index == first
        sel = jnp.maximum(sel, jnp.where(hit, jnp.where(mx > 0.5 * NEG, 1.0, 0.0), 0.0))
        score = jnp.where(hit, NEG, score)
    return sel


def _moba_prompt_kernel(q_ref, k_ref, v_ref, bias_ref, o_ref, kmean_ref, m_ref, l_ref, acc_ref,
                        *, n_blocks, n_bias):
    i = pl.program_id(2)
    bs = MOBA_BLOCK
    nt = (((1,), (1,)), ((), ()))

    @pl.when(i == 0)
    def _():
        kmean_ref[...] = jnp.zeros(kmean_ref.shape, F32)
        kmean_ref[0:n_blocks, :] = jnp.mean(k_ref[...].reshape(n_blocks, bs, ATT_HD), axis=1)

    q = q_ref[...]
    km1, km2, km3 = _split3(kmean_ref[...])
    score = (lax.dot_general(q, km1, nt, preferred_element_type=F32)
             + lax.dot_general(q, km2, nt, preferred_element_type=F32)
             + lax.dot_general(q, km3, nt, preferred_element_type=F32))
    lane = lax.broadcasted_iota(I32, score.shape, 1)
    sel = _top_mask(jnp.where(lane < i, score, NEG), lane, MOBA_TOPK, axis=1)

    def block_logits(kb, bias_idx):
        rows = pl.ds(pl.multiple_of(kb * bs, bs), bs)
        s = lax.dot_general(q, k_ref[rows, :].astype(BF16), nt, preferred_element_type=F32)
        return s * ATT_SCALE + bias_ref[bias_idx], v_ref[rows, :].astype(BF16)

    s, vb = block_logits(i, 0)
    r_idx = lax.broadcasted_iota(I32, s.shape, 0)
    c_idx = lax.broadcasted_iota(I32, s.shape, 1)
    s = jnp.where(c_idx <= r_idx, s, NEG)
    m0 = jnp.max(s, axis=-1, keepdims=True)
    p = jnp.exp(s - m0)
    m_ref[...] = m0
    l_ref[...] = jnp.sum(p, axis=-1, keepdims=True)
    acc_ref[...] = _dot(p.astype(BF16), vb)

    for d in range(1, n_blocks):
        @pl.when(d <= i)
        def _(d=d):
            kb = i - d
            s, vb = block_logits(kb, min(d, n_bias - 1))
            picked = jnp.max(jnp.where(lane == kb, sel, 0.0), axis=-1, keepdims=True)
            s = jnp.where(picked > 0.5, s, NEG)
            m_old = m_ref[...]
            m_new = jnp.maximum(m_old, jnp.max(s, axis=-1, keepdims=True))
            alpha = jnp.exp(m_old - m_new)
            p = jnp.exp(s - m_new)
            l_ref[...] = alpha * l_ref[...] + jnp.sum(p, axis=-1, keepdims=True)
            acc_ref[...] = alpha * acc_ref[...] + _dot(p.astype(BF16), vb)
            m_ref[...] = m_new

    o_ref[...] = (acc_ref[...] / l_ref[...]).astype(o_ref.dtype)


def moba_prompt(q, k, v, bias_tiles):
    n, t, _ = q.shape
    bs = MOBA_BLOCK
    n_blocks = t // bs
    n_bias = bias_tiles.shape[1]
    qo = pl.BlockSpec((None, bs, ATT_HD), lambda s, h, i: (s, i, h))
    kv = pl.BlockSpec((None, t, ATT_HD), lambda s, h, i: (s, 0, h))
    kern = functools.partial(_moba_prompt_kernel, n_blocks=n_blocks, n_bias=n_bias)
    return pl.pallas_call(
        kern,
        out_shape=jax.ShapeDtypeStruct((n, t, ATT_W), BF16),
        grid=(n, ATT_HEADS, n_blocks),
        in_specs=[qo, kv, kv,
                  pl.BlockSpec((None, n_bias, bs, bs), lambda s, h, i: (h, 0, 0, 0))],
        out_specs=qo,
        scratch_shapes=[pltpu.VMEM((LANES, ATT_HD), F32), pltpu.VMEM((bs, 1), F32),
                        pltpu.VMEM((bs, 1), F32), pltpu.VMEM((bs, ATT_HD), F32)],
        compiler_params=pltpu.CompilerParams(
            dimension_semantics=("arbitrary", "arbitrary", "arbitrary"),
            vmem_limit_bytes=_vmem_limit(4 * _nbytes((t, ATT_HD), F32),
                                         2 * _nbytes((n_bias, bs, bs), F32),
                                         16 * _nbytes((bs, bs), F32))),
        name="moba_prompt",
    )(q, k, v, bias_tiles)


def _moba_sample_kernel(pt_ref, qbd_ref, kp_ref, vp_ref, kn_ref, vn_ref, bias_ref, o_ref,
                        lg_ref, p_ref, acc_ref, *, n_pages, n_new):
    del pt_ref
    j = pl.program_id(1)
    page = kp_ref.shape[0]
    past = n_pages * page
    n_blocks = past // MOBA_BLOCK
    rows_all = lg_ref.shape[0]

    @pl.when(j < n_pages)
    def _():
        lg_ref[pl.ds(pl.multiple_of(j * page, page), page), :] = _dot(kp_ref[...].astype(BF16), qbd_ref[...])

    @pl.when(j == n_pages - 1)
    def _():
        lg_ref[past:rows_all, :] = _dot(kn_ref[...].astype(BF16), qbd_ref[...])
        score = jnp.mean(lg_ref[0:past, :].reshape(n_blocks, MOBA_BLOCK, LANES), axis=1)
        sub = lax.broadcasted_iota(I32, score.shape, 0)
        sel = _top_mask(score, sub, MOBA_TOPK, axis=0)
        for kb in range(n_blocks):
            rows = slice(kb * MOBA_BLOCK, (kb + 1) * MOBA_BLOCK)
            s = lg_ref[rows, :] * ATT_SCALE + bias_ref[rows, :]
            lg_ref[rows, :] = jnp.where(sel[kb:kb + 1, :] > 0.5, s, NEG)
        s = lg_ref[past:rows_all, :] * ATT_SCALE + bias_ref[past:rows_all, :]
        jn = lax.broadcasted_iota(I32, s.shape, 0)
        qi = lax.broadcasted_iota(I32, s.shape, 1) & (n_new - 1)
        lg_ref[past:rows_all, :] = jnp.where((jn < n_new) & (jn <= qi), s, NEG)
        lg = lg_ref[...]
        p = jnp.exp(lg - jnp.max(lg, axis=0, keepdims=True))
        p_ref[...] = p / jnp.sum(p, axis=0, keepdims=True)

    @pl.when(j == n_pages)
    def _():
        acc_ref[...] = jnp.zeros(acc_ref.shape, F32)

    @pl.when(j >= n_pages)
    def _():
        rows = pl.ds(pl.multiple_of((j - n_pages) * page, page), page)
        acc_ref[...] += _dot(p_ref[rows, :].T.astype(BF16), vp_ref[...].astype(BF16))

    @pl.when(j == 2 * n_pages - 1)
    def _():
        acc = acc_ref[...] + _dot(p_ref[past:rows_all, :].T.astype(BF16), vn_ref[...].astype(BF16))
        for h in range(ATT_HEADS):
            o_ref[:, h * ATT_HD:(h + 1) * ATT_HD] = acc[h * n_new:(h + 1) * n_new,
                                                        h * ATT_HD:(h + 1) * ATT_HD]


def moba_sample(qbd, cache_k, cache_v, k_new, v_new, page_table, bias_t):
    n_seq, n_pages = page_table.shape
    page = cache_k.shape[1]
    n_new = LANES // ATT_HEADS
    rows_all = n_pages * page + LANES
    kern = functools.partial(_moba_sample_kernel, n_pages=n_pages, n_new=n_new)
    last = n_pages - 1
    grid_spec = pltpu.PrefetchScalarGridSpec(
        num_scalar_prefetch=1,
        grid=(n_seq, 2 * n_pages),
        in_specs=[
            pl.BlockSpec((None, ATT_W, LANES), lambda s, j, pt: (s, 0, 0)),
            pl.BlockSpec((None, page, ATT_W),
                         lambda s, j, pt: (pt[s * n_pages + jnp.minimum(j, last)], 0, 0)),
            pl.BlockSpec((None, page, ATT_W),
                         lambda s, j, pt: (pt[s * n_pages + jnp.maximum(j - n_pages, 0)], 0, 0)),
            pl.BlockSpec((None, LANES, ATT_W), lambda s, j, pt: (s, 0, 0)),
            pl.BlockSpec((None, LANES, ATT_W), lambda s, j, pt: (s, 0, 0)),
            pl.BlockSpec((rows_all, LANES), lambda s, j, pt: (0, 0)),
        ],
        out_specs=pl.BlockSpec((None, n_new, ATT_W), lambda s, j, pt: (s, 0, 0)),
        scratch_shapes=[pltpu.VMEM((rows_all, LANES), F32), pltpu.VMEM((rows_all, LANES), F32),
                        pltpu.VMEM((LANES, ATT_W), F32)],
    )
    return pl.pallas_call(
        kern,
        out_shape=jax.ShapeDtypeStruct((n_seq, n_new, ATT_W), F32),
        grid_spec=grid_spec,
        compiler_params=pltpu.CompilerParams(
            dimension_semantics=("arbitrary", "arbitrary"),
            vmem_limit_bytes=_vmem_limit(12 * _nbytes((page, ATT_W), F32),
                                         8 * _nbytes((rows_all, LANES), F32))),
        name="moba_sample",
    )(page_table.reshape(-1), qbd, cache_k, cache_v, k_new, v_new, bias_t)


def _router_kernel(x_ref, g_ref, rw_ref, rb_ref, tril_ref, hn_ref, gate_ref, e_ref, rank_ref,
                   cnt_ref, run_ref):
    @pl.when(pl.program_id(0) == 0)
    def _():
        run_ref[...] = jnp.zeros(run_ref.shape, F32)

    x = x_ref[...]
    hn = x * lax.rsqrt(jnp.mean(x * x, axis=-1, keepdims=True) + RMS_EPS) * g_ref[...]
    hn_ref[...] = hn
    a1, a2, a3 = _split3(hn)
    b1, b2, b3 = _split3(rw_ref[...])
    logits = (_dot(a1, b1) + _dot(a1, b2) + _dot(a2, b1) + _dot(a2, b2) + _dot(a1, b3)
              + _dot(a3, b1)) + rb_ref[...]
    lane = lax.broadcasted_iota(I32, logits.shape, 1).astype(F32)
    onehot = jnp.zeros(logits.shape, F32)
    vals, idxs = [], []
    s = logits
    for _ in range(TOP_K):
        mx = jnp.max(s, axis=-1, keepdims=True)
        first = jnp.min(jnp.where(s == mx, lane, BIG_INDEX), axis=-1, keepdims=True)
        hit = lane == first
        onehot = jnp.where(hit, 1.0, onehot)
        s = jnp.where(hit, NEG, s)
        vals.append(mx)
        idxs.append(first)
    ex = [jnp.exp(v - vals[0]) for v in vals]
    den = ex[0] + ex[1] + ex[2] + ex[3]
    before = _dot(tril_ref[...], onehot.astype(BF16)) + run_ref[0:1, :]
    gate = jnp.zeros(logits.shape, F32)
    e_out = jnp.zeros(logits.shape, I32)
    rank = jnp.zeros(logits.shape, I32)
    for kk in range(TOP_K):
        slot = lane == kk
        rk = jnp.sum(jnp.where(lane == idxs[kk], before, 0.0), axis=-1, keepdims=True)
        gate = jnp.where(slot, ex[kk] / den, gate)
        e_out = jnp.where(slot, idxs[kk].astype(I32), e_out)
        rank = jnp.where(slot, rk.astype(I32), rank)
    gate_ref[...] = gate
    e_ref[...] = e_out
    rank_ref[...] = rank
    run_ref[...] = run_ref[...] + jnp.sum(onehot, axis=0, keepdims=True)
    cnt_ref[...] = run_ref[...]


def router(x, norm_g, router_w_pad, router_b_pad, tb):
    t, d = x.shape
    tril = (lax.broadcasted_iota(I32, (tb, tb), 0) > lax.broadcasted_iota(I32, (tb, tb), 1)).astype(BF16)
    row = lambda width: pl.BlockSpec((tb, width), lambda i: (i, 0))
    const = lambda shape: pl.BlockSpec(shape, lambda i: (0, 0))
    return pl.pallas_call(
        _router_kernel,
        out_shape=[jax.ShapeDtypeStruct((t, d), F32), jax.ShapeDtypeStruct((t, LANES), F32),
                   jax.ShapeDtypeStruct((t, LANES), I32), jax.ShapeDtypeStruct((t, LANES), I32),
                   jax.ShapeDtypeStruct((SUBLANES, LANES), F32)],
        grid=(t // tb,),
        in_specs=[row(d), const((1, d)), const((d, LANES)), const((1, LANES)), const((tb, tb))],
        out_specs=[row(d), row(LANES), row(LANES), row(LANES), const((SUBLANES, LANES))],
        scratch_shapes=[pltpu.VMEM((SUBLANES, LANES), F32)],
        compiler_params=pltpu.CompilerParams(
            dimension_semantics=("arbitrary",),
            vmem_limit_bytes=_vmem_limit(10 * _nbytes((tb, d), F32), 8 * _nbytes((d, LANES), F32))),
        name="router",
    )(x, norm_g.reshape(1, d), router_w_pad, router_b_pad, tril)


def _row_copy(src_hbm, buf, sem, src_row, dst_row):
    return pltpu.make_async_copy(src_hbm.at[pl.ds(src_row, 1), :], buf.at[pl.ds(dst_row, 1), :], sem)


def _gather_rows_kernel(idx_ref, src_hbm, o_ref, buf, sem):
    n = buf.shape[0]

    def issue(r, c):
        _row_copy(src_hbm, buf, sem, idx_ref[0, 0, r], r).start()
        return c

    def drain(r, c):
        _row_copy(src_hbm, buf, sem, 0, r).wait()
        return c

    lax.fori_loop(0, n, issue, 0)
    lax.fori_loop(0, n, drain, 0)
    o_ref[...] = buf[...].astype(o_ref.dtype)


def gather_rows(src, row_idx, tm, out_dtype):
    n_rows = row_idx.shape[0]
    d = src.shape[1]
    return pl.pallas_call(
        _gather_rows_kernel,
        out_shape=jax.ShapeDtypeStruct((n_rows, d), out_dtype),
        grid=(n_rows // tm,),
        in_specs=[pl.BlockSpec((1, 1, tm), lambda i: (i, 0, 0), memory_space=pltpu.SMEM),
                  pl.BlockSpec(memory_space=pl.ANY)],
        out_specs=pl.BlockSpec((tm, d), lambda i: (i, 0)),
        scratch_shapes=[pltpu.VMEM((tm, d), src.dtype), pltpu.SemaphoreType.DMA],
        compiler_params=pltpu.CompilerParams(
            dimension_semantics=("arbitrary",),
            vmem_limit_bytes=_vmem_limit(2 * _nbytes((tm, d), F32), 2 * _nbytes((tm, d), out_dtype))),
        name="gather_rows",
    )(row_idx.reshape(n_rows // tm, 1, tm), src)


def _combine_kernel(idx_ref, y_hbm, x_ref, gate_ref, o_ref, buf, sem, *, tb):
    n = buf.shape[0]

    def issue(r, c):
        _row_copy(y_hbm, buf, sem, idx_ref[0, 0, r], r).start()
        return c

    def drain(r, c):
        _row_copy(y_hbm, buf, sem, 0, r).wait()
        return c

    lax.fori_loop(0, n, issue, 0)
    lax.fori_loop(0, n, drain, 0)
    gate = gate_ref[...]
    ffn = gate[:, 0:1] * buf[0:tb, :]
    for kk in range(1, TOP_K):
        ffn = ffn + gate[:, kk:kk + 1] * buf[kk * tb:(kk + 1) * tb, :]
    o_ref[...] = x_ref[...] + ffn


def moe_combine(yb, x, gate, dest, tb):
    t, d = x.shape
    idx = dest.reshape(t // tb, tb, TOP_K).transpose(0, 2, 1).reshape(t // tb, 1, TOP_K * tb)
    kern = functools.partial(_combine_kernel, tb=tb)
    return pl.pallas_call(
        kern,
        out_shape=jax.ShapeDtypeStruct((t, d), F32),
        grid=(t // tb,),
        in_specs=[pl.BlockSpec((1, 1, TOP_K * tb), lambda i: (i, 0, 0), memory_space=pltpu.SMEM),
                  pl.BlockSpec(memory_space=pl.ANY),
                  pl.BlockSpec((tb, d), lambda i: (i, 0)),
                  pl.BlockSpec((tb, LANES), lambda i: (i, 0))],
        out_specs=pl.BlockSpec((tb, d), lambda i: (i, 0)),
        scratch_shapes=[pltpu.VMEM((TOP_K * tb, d), F32), pltpu.SemaphoreType.DMA],
        compiler_params=pltpu.CompilerParams(
            dimension_semantics=("arbitrary",),
            vmem_limit_bytes=_vmem_limit((TOP_K + 6) * _nbytes((tb, d), F32))),
        name="moe_combine",
    )(idx, yb, x, gate)


def _first_of_expert(be_ref, i):
    return (i == 0) | (be_ref[i] != be_ref[jnp.maximum(i - 1, 0)])


def _moe_gu_kernel(be_ref, nu_ref, x_ref, wg_ref, wu_ref, bg_ref, bu_ref, o_ref, wgb_ref, wub_ref):
    i = pl.program_id(1)
    used = i < nu_ref[0]

    @pl.when(used & _first_of_expert(be_ref, i))
    def _():
        wgb_ref[...] = wg_ref[...].astype(BF16)
        wub_ref[...] = wu_ref[...].astype(BF16)

    @pl.when(used)
    def _():
        x = x_ref[...]
        g = jnp.minimum(_dot(x, wgb_ref[...]) + bg_ref[...], SWIGLU_LIMIT)
        u = jnp.clip(_dot(x, wub_ref[...]) + bu_ref[...], -SWIGLU_LIMIT, SWIGLU_LIMIT)
        o_ref[...] = ((u + 1.0) * (g * jax.nn.sigmoid(SWIGLU_ALPHA * g))).astype(o_ref.dtype)

    @pl.when(jnp.logical_not(used))
    def _():
        o_ref[...] = jnp.zeros(o_ref.shape, o_ref.dtype)


def moe_gate_up(xb, w_gu, b_gu, block_e, n_used, n_blocks):
    n_rows, d = xb.shape
    tm, tn = MOE_TM, MOE_TN_GU
    nj = D_FF // tn
    blk = lambda i, nu: jnp.minimum(i, nu[0] - 1)
    grid_spec = pltpu.PrefetchScalarGridSpec(
        num_scalar_prefetch=2,
        grid=(nj, n_blocks),
        in_specs=[
            pl.BlockSpec((tm, d), lambda j, i, be, nu: (blk(i, nu), 0)),
            pl.BlockSpec((None, d, tn), lambda j, i, be, nu: (be[blk(i, nu)], 0, j)),
            pl.BlockSpec((None, d, tn), lambda j, i, be, nu: (be[blk(i, nu)], 0, nj + j)),
            pl.BlockSpec((None, 1, tn), lambda j, i, be, nu: (be[blk(i, nu)], 0, j)),
            pl.BlockSpec((None, 1, tn), lambda j, i, be, nu: (be[blk(i, nu)], 0, nj + j)),
        ],
        out_specs=pl.BlockSpec((tm, tn), lambda j, i, be, nu: (i, j)),
        scratch_shapes=[pltpu.VMEM((d, tn), BF16), pltpu.VMEM((d, tn), BF16)],
    )
    return pl.pallas_call(
        _moe_gu_kernel,
        out_shape=jax.ShapeDtypeStruct((n_rows, D_FF), BF16),
        grid_spec=grid_spec,
        compiler_params=pltpu.CompilerParams(
            dimension_semantics=("arbitrary", "arbitrary"),
            vmem_limit_bytes=_vmem_limit(2 * _nbytes((tm, d), BF16), 4 * _nbytes((d, tn), F32),
                                         2 * _nbytes((d, tn), BF16), 8 * _nbytes((tm, tn), F32))),
        name="moe_gate_up",
    )(block_e, n_used, xb, w_gu, w_gu, b_gu.reshape(N_EXPERTS, 1, 2 * D_FF),
      b_gu.reshape(N_EXPERTS, 1, 2 * D_FF))


def _moe_down_kernel(be_ref, nu_ref, a_ref, w_ref, b_ref, o_ref, wb_ref):
    i = pl.program_id(1)
    used = i < nu_ref[0]

    @pl.when(used & _first_of_expert(be_ref, i))
    def _():
        wb_ref[...] = w_ref[...].astype(BF16)

    @pl.when(used)
    def _():
        o_ref[...] = _dot(a_ref[...], wb_ref[...]) + b_ref[...]

    @pl.when(jnp.logical_not(used))
    def _():
        o_ref[...] = jnp.zeros(o_ref.shape, o_ref.dtype)


def moe_down(act, w_down, b_down, block_e, n_used, n_blocks):
    n_rows, dff = act.shape
    tm, tn = MOE_TM, MOE_TN_DOWN
    blk = lambda i, nu: jnp.minimum(i, nu[0] - 1)
    grid_spec = pltpu.PrefetchScalarGridSpec(
        num_scalar_prefetch=2,
        grid=(D_MODEL // tn, n_blocks),
        in_specs=[
            pl.BlockSpec((tm, dff), lambda j, i, be, nu: (blk(i, nu), 0)),
            pl.BlockSpec((None, dff, tn), lambda j, i, be, nu: (be[blk(i, nu)], 0, j)),
            pl.BlockSpec((None, 1, tn), lambda j, i, be, nu: (be[blk(i, nu)], 0, j)),
        ],
        out_specs=pl.BlockSpec((tm, tn), lambda j, i, be, nu: (i, j)),
        scratch_shapes=[pltpu.VMEM((dff, tn), BF16)],
    )
    return pl.pallas_call(
        _moe_down_kernel,
        out_shape=jax.ShapeDtypeStruct((n_rows, D_MODEL), F32),
        grid_spec=grid_spec,
        compiler_params=pltpu.CompilerParams(
            dimension_semantics=("arbitrary", "arbitrary"),
            vmem_limit_bytes=_vmem_limit(2 * _nbytes((tm, dff), BF16), 2 * _nbytes((dff, tn), F32),
                                         _nbytes((dff, tn), BF16), 4 * _nbytes((tm, tn), F32))),
        name="moe_down",
    )(block_e, n_used, act, w_down, b_down.reshape(N_EXPERTS, 1, D_MODEL))


def _t5_bucket(dist):
    n = jnp.maximum(dist, 0)
    max_exact = N_BUCKETS // 2
    nf = jnp.maximum(n, 1).astype(F32)
    large = max_exact + (jnp.log(nf / max_exact) / math.log(MAX_DISTANCE / max_exact)
                         * (N_BUCKETS - max_exact)).astype(I32)
    return jnp.where(n < max_exact, n, jnp.minimum(large, N_BUCKETS - 1))


def _bias_by_distance(rel_bias, max_dist):
    return rel_bias.T[:, _t5_bucket(jnp.arange(max_dist, dtype=I32))].astype(F32)


def _state_to_tiles(s):
    n = s.shape[0]
    s = s.reshape(n, RW_LANE_TILES, LANES // RW_HD, RW_HD, RW_HD).transpose(0, 1, 3, 2, 4)
    return s.reshape(n, RW_LANE_TILES, RW_HD, LANES)


def _tiles_to_state(s):
    n = s.shape[0]
    s = s.reshape(n, RW_LANE_TILES, RW_HD, LANES // RW_HD, RW_HD).transpose(0, 1, 3, 2, 4)
    return s.reshape(n, RW_HEADS, RW_HD, RW_HD)


def _pad_rows(w, row0, n_rows):
    return jnp.zeros((n_rows, w.shape[1]), BF16).at[row0:row0 + w.shape[0]].set(w.astype(BF16))


def kernel(x_prompt, x_sample, cache_k, cache_v, page_table, state_rwkv, state_shift, norm1_g, w_in,
           q_norm_g, k_norm_g, rel_bias, mu_shift, w0, w_lora_up, a0, a_lora_up, g_lora_up, k_k, k_a,
           r_k, lnx_w, lnx_b, w_branch_a, w_branch_b, w_out, norm2_g, router_w, router_b, w_gu, b_gu,
           w_down, b_down):
    n_p, t_p, d = x_prompt.shape
    n_s, t_s, _ = x_sample.shape
    tok_p, tok_s = n_p * t_p, n_s * t_s
    n_tok = tok_p + tok_s
    x = jnp.concatenate([x_prompt.reshape(tok_p, d), x_sample.reshape(tok_s, d)], axis=0)

    h = rmsnorm_rows(x, norm1_g, BF16, 256)
    cb = ATT_W // MM_TN
    p_q = matmul(h, w_in, ATT_W, col_block0=0, name="proj_q")
    p_k = matmul(h, w_in, ATT_W, col_block0=cb, name="proj_k")
    p_v = matmul(h, w_in, ATT_W, col_block0=2 * cb, name="proj_v")
    p_rw = matmul(h, w_in, RW_PAD, col_block0=3 * cb, name="proj_rw")
    w_gate = w_in[:, 3 * ATT_W + RW_COLS:]
    p_ga = matmul(h, w_gate, D_MODEL, col_block0=0, name="proj_gate_a")
    p_gb = matmul(h, w_gate, D_MODEL, col_block0=D_MODEL // MM_TN, name="proj_gate_b")

    head_rows = n_tok * ATT_HEADS
    q_n = rmsnorm_rows(p_q.reshape(head_rows, ATT_HD), q_norm_g, BF16, 2048).reshape(n_tok, ATT_W)
    k_n = rmsnorm_rows(p_k.reshape(head_rows, ATT_HD), k_norm_g, F32, 2048).reshape(n_tok, ATT_W)

    past = page_table.shape[1] * cache_k.shape[1]
    bias_d = _bias_by_distance(rel_bias, max(t_p, past + t_s))
    n_bias = 6
    ri = jnp.arange(MOBA_BLOCK, dtype=I32)
    dist = (jnp.arange(n_bias, dtype=I32)[:, None, None] * MOBA_BLOCK + ri[None, :, None] - ri[None, None, :])
    bias_tiles = bias_d[:, jnp.maximum(dist, 0)]
    o_att_p = moba_prompt(q_n[:tok_p].reshape(n_p, t_p, ATT_W), k_n[:tok_p].reshape(n_p, t_p, ATT_W),
                          p_v[:tok_p].reshape(n_p, t_p, ATT_W), bias_tiles)

    q_s = q_n[tok_p:].reshape(n_s, t_s, ATT_HEADS, ATT_HD)
    eye = jnp.eye(ATT_HEADS, dtype=BF16)
    qbd = (q_s.transpose(0, 2, 3, 1)[:, :, :, None, :] * eye[None, :, None, :, None]
           ).reshape(n_s, ATT_W, ATT_HEADS * t_s)
    pad_new = lambda a: jnp.pad(a[tok_p:].reshape(n_s, t_s, ATT_W), ((0, 0), (0, LANES - t_s), (0, 0)))
    key_pos = jnp.arange(past + LANES, dtype=I32)
    q_idx = jnp.arange(t_s, dtype=I32)
    d_s = jnp.clip(past + q_idx[None, :] - key_pos[:, None], 0, past + t_s - 1)
    bias_t = bias_d[:, d_s].transpose(1, 0, 2).reshape(past + LANES, ATT_HEADS * t_s)
    pool = cache_k.shape[0]
    o_att_s = moba_sample(qbd, cache_k.reshape(pool, -1, ATT_W), cache_v.reshape(pool, -1, ATT_W),
                          pad_new(k_n), pad_new(p_v), page_table, bias_t)
    o_att = jnp.concatenate([o_att_p.reshape(tok_p, ATT_W), o_att_s.reshape(tok_s, ATT_W).astype(BF16)], axis=0)

    rw_p = p_rw[:tok_p].reshape(n_p, t_p, RW_PAD)
    rw_s = p_rw[tok_p:].reshape(n_s, t_s, RW_PAD)
    shift_s = jnp.pad(state_shift, ((0, 0), (0, RW_PAD - RW_COLS)))
    prev = jnp.concatenate([
        jnp.concatenate([jnp.zeros((n_p, 1, RW_PAD), F32), rw_p[:, :-1]], axis=1).reshape(tok_p, RW_PAD),
        jnp.concatenate([shift_s[:, None, :], rw_s[:, :-1]], axis=1).reshape(tok_s, RW_PAD)], axis=0)
    row = lambda a: a.reshape(1, -1).astype(F32)
    vecs = (jnp.pad(row(mu_shift), ((0, 0), (0, RW_PAD - RW_COLS))), row(w0), row(a0), row(k_k), row(k_a),
            row(r_k))
    loras = (_pad_rows(w_lora_up, 0, LORA_PAD), _pad_rows(a_lora_up, W_LORA, LORA_PAD),
             _pad_rows(g_lora_up, W_LORA + A_LORA, LORA_PAD))
    lane_i = jnp.arange(LANES, dtype=I32)
    bd = (lane_i[:, None] // RW_HD == lane_i[None, :] // RW_HD).astype(BF16)
    ipat = (jnp.arange(RW_HD, dtype=I32)[:, None] == lane_i[None, :] % RW_HD).astype(F32)
    prep = rwkv_prep(p_rw, prev, vecs, loras, bd, 64)
    scan_in, (g_out, bonus) = prep[:8], prep[8:]
    tiles = lambda a, lo, n, t: a[lo:lo + n * t].reshape(n, t, RW_LANE_TILES, LANES)
    y_p, s_p = rwkv_scan([tiles(a, 0, n_p, t_p) for a in scan_in],
                         jnp.zeros((n_p, RW_LANE_TILES, RW_HD, LANES), F32), bd, ipat, n_p, 64)
    y_s, s_s = rwkv_scan([tiles(a, tok_p, n_s, t_s) for a in scan_in],
                         _state_to_tiles(state_rwkv.astype(F32)), bd, ipat, 4, t_s)
    y = jnp.concatenate([y_p.reshape(tok_p, RW_W), y_s.reshape(tok_s, RW_W)], axis=0)
    o_rw = rwkv_post(y, g_out, bonus, row(lnx_w), row(lnx_b), bd, 256)

    m_a = matmul(o_att, w_branch_a, D_MODEL, epilogue="gate", gate=p_ga, name="branch_a")
    merged = matmul(o_rw, w_branch_b, D_MODEL, epilogue="gate", gate=p_gb, addend=m_a,
                    out_dtype=BF16, name="branch_b")
    x1 = matmul(merged, w_out, D_MODEL, epilogue="residual", addend=x, name="out_proj")

    rw_pad = jnp.pad(router_w.astype(F32), ((0, 0), (0, LANES - N_EXPERTS)))
    rb_pad = jnp.pad(router_b.astype(F32).reshape(1, -1), ((0, 0), (0, LANES - N_EXPERTS)),
                     constant_values=NEG)
    hn, gate, e_sel, rank, counts = router(x1, norm2_g, rw_pad, rb_pad, 256)
    tm = MOE_TM
    n_assign = n_tok * TOP_K
    n_blocks = -(-(n_assign + N_EXPERTS * (tm - 1)) // tm)
    counts = counts[0, :N_EXPERTS].astype(I32)
    padded = (counts + tm - 1) // tm * tm
    pad_end = jnp.cumsum(padded)
    pad_start = pad_end - padded
    dest = pad_start[e_sel[:, :TOP_K]] + rank[:, :TOP_K]
    row_tok = jnp.zeros((n_blocks * tm,), I32).at[dest.reshape(-1)].set(
        jnp.arange(n_assign, dtype=I32) // TOP_K)
    block_e = jnp.minimum(jnp.searchsorted(pad_end, jnp.arange(n_blocks, dtype=I32) * tm, side="right"),
                          N_EXPERTS - 1).astype(I32)
    n_used = (pad_end[-1:] // tm).astype(I32)
    xb = gather_rows(hn, row_tok, tm, BF16)
    act = moe_gate_up(xb, w_gu, b_gu, block_e, n_used, n_blocks)
    yb = moe_down(act, w_down, b_down, block_e, n_used, n_blocks)
    y_out = moe_combine(yb, x1, gate, dest, 128)

    lead = lambda a, lo, n, t: a[lo:lo + n * t].reshape(n, t, ATT_HEADS, ATT_HD)
    return (y_out[:tok_p].reshape(n_p, t_p, d), y_out[tok_p:].reshape(n_s, t_s, d),
            lead(k_n, 0, n_p, t_p), lead(p_v, 0, n_p, t_p),
            lead(k_n, tok_p, n_s, t_s), lead(p_v, tok_p, n_s, t_s),
            _tiles_to_state(s_p), rw_p[:, -1, :RW_COLS],
            _tiles_to_state(s_s), rw_s[:, -1, :RW_COLS])
```

```python
import functools
import math

import jax
import jax.numpy as jnp
from jax import lax
from jax.experimental import pallas as pl
from jax.experimental.pallas import tpu as pltpu

F32 = jnp.float32
BF16 = jnp.bfloat16
I32 = jnp.int32

LANES = 128
SUBLANES = 8
VMEM_CAP_BYTES = 60 * 1024 * 1024
VMEM_SLACK_BYTES = 6 * 1024 * 1024

D_MODEL = 4096
ATT_HD = 128
ATT_W = D_MODEL // 2
ATT_HEADS = ATT_W // ATT_HD
MOBA_BLOCK = 256
MOBA_TOPK = 3
ATT_SCALE = ATT_HD ** -0.5
N_BUCKETS = 32
MAX_DISTANCE = 1024
RW_HD = 64
RW_W = D_MODEL // 2
RW_HEADS = RW_W // RW_HD
W_LORA = max(32, int(round(RW_W ** 0.5 * 1.8 / 32)) * 32)
A_LORA = max(32, int(round(RW_W ** 0.5 * 1.8 / 32)) * 32)
G_LORA = max(32, int(round(RW_W ** 0.8 * 0.6 / 32)) * 32)
RW_COLS = 3 * RW_W + W_LORA + A_LORA + G_LORA
LORA_COLS = W_LORA + A_LORA + G_LORA
LNX_EPS = 64e-5
N_EXPERTS = 32
TOP_K = 4
D_FF = D_MODEL
SWIGLU_LIMIT = 7.0
SWIGLU_ALPHA = 1.702
RMS_EPS = 1e-6
NEG = -1e30
BIG_INDEX = 1e9

RW_LANE_TILES = RW_W // LANES
MM_TM = 512
MM_TN = 512
RW_PAD = -(-RW_COLS // MM_TN) * MM_TN
LORA_PAD = RW_PAD - 3 * RW_W
MOE_TM = 256
MOE_TN_GU = 512
MOE_TN_DOWN = 1024
MOBA_GROUP = 4
SAMPLE_PAGES_PER_STEP = 4
PREP_TB = 64


def _vmem_limit(*buffer_bytes):
    return int(min(VMEM_CAP_BYTES, sum(buffer_bytes) + VMEM_SLACK_BYTES))


def _nbytes(shape, dtype):
    return math.prod(shape) * jnp.dtype(dtype).itemsize


def _split3(x):
    x1 = x.astype(BF16)
    r1 = x - x1.astype(F32)
    x2 = r1.astype(BF16)
    x3 = (r1 - x2.astype(F32)).astype(BF16)
    return x1, x2, x3


def _dot(a, b):
    return jnp.dot(a, b, preferred_element_type=F32)


def _rmsnorm_kernel(x_ref, g_ref, o_ref):
    x = x_ref[...].astype(F32)
    y = x * lax.rsqrt(jnp.mean(x * x, axis=-1, keepdims=True) + RMS_EPS)
    o_ref[...] = (y * g_ref[...]).astype(o_ref.dtype)


def rmsnorm_rows(x, g, out_dtype, tb):
    m, d = x.shape
    return pl.pallas_call(
        _rmsnorm_kernel,
        out_shape=jax.ShapeDtypeStruct((m, d), out_dtype),
        grid=(m // tb,),
        in_specs=[pl.BlockSpec((tb, d), lambda i: (i, 0)),
                  pl.BlockSpec((1, d), lambda i: (0, 0))],
        out_specs=pl.BlockSpec((tb, d), lambda i: (i, 0)),
        compiler_params=pltpu.CompilerParams(
            dimension_semantics=("arbitrary",),
            vmem_limit_bytes=_vmem_limit(4 * _nbytes((tb, d), F32))),
        name="rmsnorm_rows",
    )(x, g.reshape(1, d).astype(F32))


def _matmul_kernel(*refs, epilogue, has_addend):
    a_ref, w_ref = refs[0], refs[1]
    o_ref, wb_ref = refs[-2], refs[-1]

    @pl.when(pl.program_id(1) == 0)
    def _():
        wb_ref[...] = w_ref[...].astype(BF16)

    acc = _dot(a_ref[...], wb_ref[...])
    if epilogue == "gate":
        acc = jax.nn.sigmoid(refs[2][...]) * acc
        if has_addend:
            acc = acc + refs[3][...]
    elif epilogue == "residual":
        acc = refs[2][...] + acc
    o_ref[...] = acc.astype(o_ref.dtype)


def matmul(a, w, n_out, *, col_block0=0, epilogue=None, gate=None, addend=None,
           out_dtype=F32, name="matmul"):
    m, k = a.shape
    tm, tn = MM_TM, MM_TN
    extra, extra_specs = [], []
    io_spec = pl.BlockSpec((tm, tn), lambda j, i: (i, j))
    if epilogue == "gate":
        extra.append(gate)
        extra_specs.append(io_spec)
        if addend is not None:
            extra.append(addend)
            extra_specs.append(io_spec)
    elif epilogue == "residual":
        extra.append(addend)
        extra_specs.append(io_spec)
    kern = functools.partial(_matmul_kernel, epilogue=epilogue,
                             has_addend=(epilogue == "gate" and addend is not None))
    return pl.pallas_call(
        kern,
        out_shape=jax.ShapeDtypeStruct((m, n_out), out_dtype),
        grid=(n_out // tn, m // tm),
        in_specs=[pl.BlockSpec((tm, k), lambda j, i: (i, 0)),
                  pl.BlockSpec((k, tn), lambda j, i: (0, j + col_block0))] + extra_specs,
        out_specs=io_spec,
        scratch_shapes=[pltpu.VMEM((k, tn), BF16)],
        compiler_params=pltpu.CompilerParams(
            dimension_semantics=("arbitrary", "arbitrary"),
            vmem_limit_bytes=_vmem_limit(
                2 * _nbytes((tm, k), BF16), 2 * _nbytes((k, tn), F32), _nbytes((k, tn), BF16),
                2 * (2 + len(extra)) * _nbytes((tm, tn), F32))),
        name=name,
    )(a, w, *extra)


def _segsum(x, bd):
    outs = []
    for c in range(x.shape[-1] // LANES):
        x1, x2, x3 = _split3(x[:, c * LANES:(c + 1) * LANES])
        outs.append(_dot(x1, bd) + _dot(x2, bd) + _dot(x3, bd))
    return jnp.concatenate(outs, axis=-1)


def _softplus(z):
    return jnp.maximum(z, 0.0) + jnp.log(1.0 + jnp.exp(-jnp.abs(z)))


def _rwkv_prep_kernel(rw_ref, shift_ref, mu_ref, w0_ref, a0_ref, kk_g_ref, ka_ref, rk_ref,
                      wl_ref, al_ref, gl_ref, bd_ref,
                      w_o, r_o, a_o, b_o, kx_o, v_o, br_o, kr_o, g_o, bonus_o, carry_ref, *, seq_len):
    rw = rw_ref[...]
    tb = rw.shape[0]
    row = lax.broadcasted_iota(I32, rw.shape, 0)
    prev = pltpu.roll(rw, 1, axis=0)
    if seq_len >= tb:
        @pl.when(pl.program_id(0) == 0)
        def _():
            carry_ref[...] = jnp.zeros(carry_ref.shape, F32)

        starts = pl.program_id(0) % (seq_len // tb) == 0
        first = jnp.where(starts, shift_ref[...], carry_ref[...])
        prev = jnp.where(row == 0, first, prev)
        carry_ref[...] = rw[tb - 1:tb, :]
    else:
        for s in range(tb // seq_len):
            prev = jnp.where(row == s * seq_len, shift_ref[s:s + 1, :], prev)
    m = rw + (prev - rw) * mu_ref[...]
    r = m[:, 0:RW_W]
    k = m[:, RW_W:2 * RW_W]
    v = m[:, 2 * RW_W:3 * RW_W]
    x = m[:, 3 * RW_W:]
    bd = bd_ref[...]
    u = w0_ref[...] + _dot(jnp.tanh(x).astype(BF16), wl_ref[...])
    w_log = -_softplus(-u) - 0.5
    decay = jnp.exp(-jnp.exp(w_log))
    a = jax.nn.sigmoid(a0_ref[...] + _dot(x.astype(BF16), al_ref[...]))
    g = _dot(jax.nn.sigmoid(x).astype(BF16), gl_ref[...])
    kk = k * kk_g_ref[...]
    kk = kk / jnp.maximum(jnp.sqrt(_segsum(kk * kk, bd)), 1e-12)
    kx = k * (1.0 + (a - 1.0) * ka_ref[...])
    bvec = kk * a
    w_o[...] = decay
    r_o[...] = r
    a_o[...] = -kk
    b_o[...] = bvec
    kx_o[...] = kx
    v_o[...] = v
    br_o[...] = _segsum(bvec * r, bd)
    kr_o[...] = _segsum(kx * r, bd)
    g_o[...] = g
    bonus_o[...] = _segsum(r * kx * rk_ref[...], bd) * v


def rwkv_prep(rw, row0, n_seq, seq_len, shift, vecs, loras, bd):
    tb = PREP_TB
    t = n_seq * seq_len
    blk0 = row0 // tb
    const = lambda shape: pl.BlockSpec(shape, lambda i: (0, 0))
    if seq_len >= tb:
        per_seq = seq_len // tb
        shift = shift.reshape(n_seq, 1, RW_PAD)
        shift_spec = pl.BlockSpec((None, 1, RW_PAD), lambda i: (i // per_seq, 0, 0))
    else:
        shift_spec = pl.BlockSpec((tb // seq_len, RW_PAD), lambda i: (i, 0))
    out_spec = pl.BlockSpec((tb, RW_W), lambda i: (i, 0))
    mu, w0, a0, k_k, k_a, r_k = vecs
    n_out = 10
    return pl.pallas_call(
        functools.partial(_rwkv_prep_kernel, seq_len=seq_len),
        out_shape=[jax.ShapeDtypeStruct((t, RW_W), F32)] * n_out,
        grid=(t // tb,),
        in_specs=[pl.BlockSpec((tb, RW_PAD), lambda i: (i + blk0, 0)), shift_spec, const((1, RW_PAD))]
                 + [const((1, RW_W))] * 5 + [const((LORA_PAD, RW_W))] * 3 + [const((LANES, LANES))],
        out_specs=[out_spec] * n_out,
        scratch_shapes=[pltpu.VMEM((1, RW_PAD), F32)],
        compiler_params=pltpu.CompilerParams(
            dimension_semantics=("arbitrary",),
            vmem_limit_bytes=_vmem_limit(6 * _nbytes((tb, RW_PAD), F32),
                                         2 * n_out * _nbytes((tb, RW_W), F32),
                                         6 * _nbytes((LORA_PAD, RW_W), BF16),
                                         16 * _nbytes((tb, RW_W), F32))),
        name="rwkv_prep",
    )(rw, shift, mu, w0, a0, k_k, k_a, r_k, *loras, bd)


def _rwkv_scan_kernel(w_ref, r_ref, a_ref, b_ref, kx_ref, v_ref, br_ref, kr_ref, s0_ref, bd_ref,
                      ipat_ref, y_ref, s_ref, lhs_ref, res_ref, *, n_group, n_tok):
    @pl.when(pl.program_id(1) == 0)
    def _():
        s_ref[...] = s0_ref[...]

    ipat = ipat_ref[...]
    seg = 3 * RW_HD

    def step(t, carry):
        rows = []
        for g in range(n_group):
            w, r, a = w_ref[g, t], r_ref[g, t], a_ref[g, t]
            wr = w * r
            vv = v_ref[g, t]
            for c in range(RW_LANE_TILES):
                s = s_ref[g, c]
                base = (g * RW_LANE_TILES + c) * seg
                lhs_ref[base:base + RW_HD, :] = (s * a[c:c + 1]).astype(BF16)
                lhs_ref[base + RW_HD:base + 2 * RW_HD, :] = (s * wr[c:c + 1]).astype(BF16)
                lhs_ref[base + 2 * RW_HD:base + seg, :] = (ipat * vv[c:c + 1]).astype(BF16)
            rows.append((w, r))
        res_ref[...] = _dot(lhs_ref[...], bd_ref[...])
        for g in range(n_group):
            w = rows[g][0]
            b, kx, br, kr = b_ref[g, t], kx_ref[g, t], br_ref[g, t], kr_ref[g, t]
            for c in range(RW_LANE_TILES):
                base = (g * RW_LANE_TILES + c) * seg
                sa = res_ref[base:base + RW_HD, :]
                swr = res_ref[base + RW_HD:base + 2 * RW_HD, :]
                vx = res_ref[base + 2 * RW_HD:base + seg, :]
                s_ref[g, c] = s_ref[g, c] * w[c:c + 1] + sa * b[c:c + 1] + vx * kx[c:c + 1]
                y = swr + sa * br[c:c + 1] + vx * kr[c:c + 1]
                y_ref[g, t, pl.ds(c, 1), :] = jnp.sum(y * ipat, axis=0, keepdims=True)
        return carry

    lax.fori_loop(0, n_tok, step, 0)


def rwkv_scan(seq_inputs, s0, bd, ipat, n_group, n_tok):
    n, t = seq_inputs[0].shape[:2]
    tok = pl.BlockSpec((n_group, n_tok, RW_LANE_TILES, LANES), lambda i, j: (i, j, 0, 0))
    st = pl.BlockSpec((n_group, RW_LANE_TILES, RW_HD, LANES), lambda i, j: (i, 0, 0, 0))
    lhs_rows = n_group * RW_LANE_TILES * 3 * RW_HD
    kern = functools.partial(_rwkv_scan_kernel, n_group=n_group, n_tok=n_tok)
    tok_bytes = _nbytes((n_group, n_tok, RW_LANE_TILES, LANES), F32)
    st_bytes = _nbytes((n_group, RW_LANE_TILES, RW_HD, LANES), F32)
    return pl.pallas_call(
        kern,
        out_shape=[jax.ShapeDtypeStruct((n, t, RW_LANE_TILES, LANES), F32),
                   jax.ShapeDtypeStruct(s0.shape, F32)],
        grid=(n // n_group, t // n_tok),
        in_specs=[tok] * 8 + [st, pl.BlockSpec((LANES, LANES), lambda i, j: (0, 0)),
                              pl.BlockSpec((RW_HD, LANES), lambda i, j: (0, 0))],
        out_specs=[tok, st],
        scratch_shapes=[pltpu.VMEM((lhs_rows, LANES), BF16), pltpu.VMEM((lhs_rows, LANES), F32)],
        compiler_params=pltpu.CompilerParams(
            dimension_semantics=("arbitrary", "arbitrary"),
            vmem_limit_bytes=_vmem_limit(18 * tok_bytes, 4 * st_bytes,
                                         _nbytes((lhs_rows, LANES), BF16),
                                         _nbytes((lhs_rows, LANES), F32))),
        name="rwkv_scan",
    )(*seq_inputs, s0, bd, ipat)


def _rwkv_post_kernel(y_ref, g_ref, bonus_ref, lw_ref, lb_ref, bd_ref, o_ref):
    y = y_ref[...]
    bd = bd_ref[...]
    mu = _segsum(y, bd) * (1.0 / RW_HD)
    d = y - mu
    var = _segsum(d * d, bd) * (1.0 / RW_HD)
    yn = d * lax.rsqrt(var + LNX_EPS) * lw_ref[...] + lb_ref[...]
    o_ref[...] = ((yn + bonus_ref[...]) * g_ref[...]).astype(o_ref.dtype)


def rwkv_post(y, g, bonus, lnx_w, lnx_b, bd, tb):
    t = y.shape[0]
    row = pl.BlockSpec((tb, RW_W), lambda i: (i, 0))
    const = lambda shape: pl.BlockSpec(shape, lambda i: (0, 0))
    return pl.pallas_call(
        _rwkv_post_kernel,
        out_shape=jax.ShapeDtypeStruct((t, RW_W), BF16),
        grid=(t // tb,),
        in_specs=[row, row, row, const((1, RW_W)), const((1, RW_W)), const((LANES, LANES))],
        out_specs=row,
        compiler_params=pltpu.CompilerParams(
            dimension_semantics=("arbitrary",),
            vmem_limit_bytes=_vmem_limit(16 * _nbytes((tb, RW_W), F32))),
        name="rwkv_post",
    )(y, g, bonus, lnx_w, lnx_b, bd)


def _top_mask(score, index, n_pick, axis):
    index = index.astype(F32)
    sel = jnp.zeros(score.shape, F32)
    for _ in range(n_pick):
        mx = jnp.max(score, axis=axis, keepdims=True)
        first = jnp.min(jnp.where(score == mx, index, BIG_INDEX), axis=axis, keepdims=True)
        hit = index == first
        sel = jnp.maximum(sel, jnp.where(hit, jnp.where(mx > 0.5 * NEG, 1.0, 0.0), 0.0))
        score = jnp.where(hit, NEG, score)
    return sel


def _moba_prompt_kernel(q_ref, k_ref, v_ref, bias_ref, expand_ref, o_ref, kmean_ref, m_ref, l_ref,
                        acc_ref, *, n_blocks, n_bias):
    i = pl.program_id(2)
    bs = MOBA_BLOCK
    nt = (((1,), (1,)), ((), ()))

    @pl.when(i == 0)
    def _():
        kmean_ref[...] = jnp.zeros(kmean_ref.shape, F32)
        kmean_ref[0:n_blocks, :] = jnp.mean(k_ref[...].reshape(n_blocks, bs, ATT_HD), axis=1)

    q = q_ref[...]
    km1, km2, km3 = _split3(kmean_ref[...])
    score = (lax.dot_general(q, km1, nt, preferred_element_type=F32)
             + lax.dot_general(q, km2, nt, preferred_element_type=F32)
             + lax.dot_general(q, km3, nt, preferred_element_type=F32))
    lane = lax.broadcasted_iota(I32, score.shape, 1)
    sel = _top_mask(jnp.where(lane < i, score, NEG), lane, MOBA_TOPK, axis=1)
    attend = jnp.where(lane == i, 1.0, sel).astype(BF16)

    m_ref[...] = jnp.full(m_ref.shape, NEG, F32)
    l_ref[...] = jnp.zeros(l_ref.shape, F32)
    acc_ref[...] = jnp.zeros(acc_ref.shape, F32)
    gw = MOBA_GROUP * bs
    q_pos = i * bs + lax.broadcasted_iota(I32, (bs, gw), 0)
    k_off = lax.broadcasted_iota(I32, (bs, gw), 1)

    for g in range(n_blocks // MOBA_GROUP):
        @pl.when(g * MOBA_GROUP <= i)
        def _(g=g):
            rows = slice(g * gw, (g + 1) * gw)
            s = lax.dot_general(q, k_ref[rows, :].astype(BF16), nt, preferred_element_type=F32) * ATT_SCALE
            bias = [bias_ref[jnp.clip(i - (g * MOBA_GROUP + u), 0, n_bias - 1)] for u in range(MOBA_GROUP)]
            s = s + jnp.concatenate(bias, axis=-1)
            on = _dot(attend, expand_ref[g])
            ok = jnp.where(on > 0.5, jnp.where(g * gw + k_off <= q_pos, 1.0, 0.0), 0.0)
            m_old = m_ref[...]
            m_new = jnp.maximum(m_old, jnp.max(jnp.where(ok > 0.5, s, NEG), axis=-1, keepdims=True))
            alpha = jnp.exp(m_old - m_new)
            p = jnp.where(ok > 0.5, jnp.exp(s - m_new), 0.0)
            l_ref[...] = alpha * l_ref[...] + jnp.sum(p, axis=-1, keepdims=True)
            acc_ref[...] = alpha * acc_ref[...] + _dot(p.astype(BF16), v_ref[rows, :].astype(BF16))
            m_ref[...] = m_new

    o_ref[...] = (acc_ref[...] / l_ref[...]).astype(o_ref.dtype)


def moba_prompt(q, k, v, bias_tiles, n, t):
    bs = MOBA_BLOCK
    n_blocks = t // bs
    n_groups = n_blocks // MOBA_GROUP
    n_bias = bias_tiles.shape[1]
    gw = MOBA_GROUP * bs
    blk_of_key = jnp.arange(n_groups * gw, dtype=I32).reshape(n_groups, 1, gw) // bs
    expand = (jnp.arange(LANES, dtype=I32)[None, :, None] == blk_of_key).astype(BF16)
    qo = pl.BlockSpec((bs, ATT_HD), lambda s, h, i: (s * n_blocks + i, h))
    kv = pl.BlockSpec((t, ATT_HD), lambda s, h, i: (s, h))
    kern = functools.partial(_moba_prompt_kernel, n_blocks=n_blocks, n_bias=n_bias)
    return pl.pallas_call(
        kern,
        out_shape=jax.ShapeDtypeStruct((n * t, ATT_W), BF16),
        grid=(n, ATT_HEADS, n_blocks),
        in_specs=[qo, kv, kv,
                  pl.BlockSpec((None, n_bias, bs, bs), lambda s, h, i: (h, 0, 0, 0)),
                  pl.BlockSpec((n_groups, LANES, gw), lambda s, h, i: (0, 0, 0))],
        out_specs=qo,
        scratch_shapes=[pltpu.VMEM((LANES, ATT_HD), F32), pltpu.VMEM((bs, 1), F32),
                        pltpu.VMEM((bs, 1), F32), pltpu.VMEM((bs, ATT_HD), F32)],
        compiler_params=pltpu.CompilerParams(
            dimension_semantics=("arbitrary", "arbitrary", "arbitrary"),
            vmem_limit_bytes=_vmem_limit(4 * _nbytes((t, ATT_HD), F32),
                                         2 * _nbytes((n_bias, bs, bs), F32),
                                         2 * _nbytes((n_groups, LANES, gw), BF16),
                                         12 * _nbytes((bs, gw), F32))),
        name="moba_prompt",
    )(q, k, v, bias_tiles, expand)


def _moba_sample_kernel(pt_ref, qall_ref, *refs, n_pages, pps, n_new):
    del pt_ref
    kp_refs, vp_refs = refs[:pps], refs[pps:2 * pps]
    kn_ref, vn_ref, bias_ref, o_ref, r_ref, lg_ref, p_ref, acc_ref = refs[2 * pps:]
    j = pl.program_id(1)
    nh = ATT_HEADS
    page = kp_refs[0].shape[0] // nh
    past = n_pages * page
    n_ksteps = n_pages // pps
    n_blocks = past // MOBA_BLOCK
    rows_all = lg_ref.shape[0]
    lane = lax.broadcasted_iota(I32, (1, LANES), 1)
    head_of_lane = lane >> (n_new.bit_length() - 1)
    head_cols = [jnp.where(head_of_lane == h, 1.0, 0.0) for h in range(nh)]

    def raw_logits(src_ref, n_tok):
        r_ref[0:n_tok * nh, :] = _dot(src_ref[...].astype(BF16), qall_ref[...])
        out = r_ref[pl.ds(0, n_tok, stride=nh), :] * head_cols[0]
        for h in range(1, nh):
            out = out + r_ref[pl.ds(h, n_tok, stride=nh), :] * head_cols[h]
        return out

    @pl.when(j < n_ksteps)
    def _():
        for u in range(pps):
            row0 = pl.multiple_of((j * pps + u) * page, page)
            lg_ref[pl.ds(row0, page), :] = raw_logits(kp_refs[u], page)

    @pl.when(j == n_ksteps - 1)
    def _():
        new = raw_logits(kn_ref, n_new) * ATT_SCALE + bias_ref[past:past + n_new, :]
        jn = lax.broadcasted_iota(I32, new.shape, 0)
        qi = lax.broadcasted_iota(I32, new.shape, 1) & (n_new - 1)
        score = jnp.mean(lg_ref[0:past, :].reshape(n_blocks, MOBA_BLOCK, LANES), axis=1)
        sub = lax.broadcasted_iota(I32, score.shape, 0)
        sel = _top_mask(score, sub, MOBA_TOPK, axis=0)
        for kb in range(n_blocks):
            rows = slice(kb * MOBA_BLOCK, (kb + 1) * MOBA_BLOCK)
            s = lg_ref[rows, :] * ATT_SCALE + bias_ref[rows, :]
            lg_ref[rows, :] = jnp.where(sel[kb:kb + 1, :] > 0.5, s, NEG)
        lg_ref[past:past + n_new, :] = jnp.where(jn <= qi, new, NEG)
        lg_ref[past + n_new:rows_all, :] = jnp.full((rows_all - past - n_new, LANES), NEG, F32)
        lg = lg_ref[...]
        p = jnp.exp(lg - jnp.max(lg, axis=0, keepdims=True))
        p_ref[...] = p / jnp.sum(p, axis=0, keepdims=True)

    @pl.when(j == n_ksteps)
    def _():
        acc_ref[...] = jnp.zeros(acc_ref.shape, F32)

    def add_pv(p_rows, v_of_head):
        pt = p_rows.T.astype(BF16)
        for h in range(nh):
            lo = (h // 2) * 2 * n_new
            o = _dot(pt[lo:lo + 2 * n_new, :], v_of_head(h))
            off = (h % 2) * n_new
            acc_ref[h * n_new:(h + 1) * n_new, :] += o[off:off + n_new, :]

    @pl.when(j >= n_ksteps)
    def _():
        for u in range(pps):
            row0 = pl.multiple_of(((j - n_ksteps) * pps + u) * page, page)
            add_pv(p_ref[pl.ds(row0, page), :],
                   lambda h, u=u: vp_refs[u][pl.ds(h, page, stride=nh), :].astype(BF16))

    @pl.when(j == 2 * n_ksteps - 1)
    def _():
        def v_new(h):
            rows = vn_ref[pl.ds(h, n_new, stride=nh), :]
            return jnp.concatenate([rows, jnp.zeros((rows_all - past - n_new, ATT_HD), F32)],
                                   axis=0).astype(BF16)

        add_pv(p_ref[past:rows_all, :], v_new)
        acc = acc_ref[...]
        for h in range(nh):
            o_ref[:, h * ATT_HD:(h + 1) * ATT_HD] = acc[h * n_new:(h + 1) * n_new, :]


def moba_sample(q_all, cache_k, cache_v, k_new, v_new, page_table, bias_t):
    n_seq, n_pages = page_table.shape
    pps = SAMPLE_PAGES_PER_STEP
    page_rows = cache_k.shape[1]
    n_new = LANES // ATT_HEADS
    past = n_pages * page_rows // ATT_HEADS
    rows_all = past + LANES
    n_ksteps = n_pages // pps
    kern = functools.partial(_moba_sample_kernel, n_pages=n_pages, pps=pps, n_new=n_new)

    def page_spec(u, for_v):
        def index(s, j, pt):
            step = jnp.maximum(j - n_ksteps, 0) if for_v else jnp.minimum(j, n_ksteps - 1)
            return (pt[s * n_pages + step * pps + u], 0, 0)
        return pl.BlockSpec((None, page_rows, ATT_HD), index)

    per_seq = lambda rows, cols: pl.BlockSpec((None, rows, cols), lambda s, j, pt: (s, 0, 0))
    grid_spec = pltpu.PrefetchScalarGridSpec(
        num_scalar_prefetch=1,
        grid=(n_seq, 2 * n_ksteps),
        in_specs=[per_seq(ATT_HD, LANES)]
                 + [page_spec(u, False) for u in range(pps)] + [page_spec(u, True) for u in range(pps)]
                 + [per_seq(n_new * ATT_HEADS, ATT_HD), per_seq(n_new * ATT_HEADS, ATT_HD),
                    pl.BlockSpec((rows_all, LANES), lambda s, j, pt: (0, 0))],
        out_specs=per_seq(n_new, ATT_W),
        scratch_shapes=[pltpu.VMEM((page_rows, LANES), F32), pltpu.VMEM((rows_all, LANES), F32),
                        pltpu.VMEM((rows_all, LANES), F32), pltpu.VMEM((LANES, ATT_HD), F32)],
    )
    return pl.pallas_call(
        kern,
        out_shape=jax.ShapeDtypeStruct((n_seq, n_new, ATT_W), F32),
        grid_spec=grid_spec,
        compiler_params=pltpu.CompilerParams(
            dimension_semantics=("arbitrary", "arbitrary"),
            vmem_limit_bytes=_vmem_limit((4 * pps + 4) * _nbytes((page_rows, ATT_HD), F32),
                                         8 * _nbytes((rows_all, LANES), F32))),
        name="moba_sample",
    )(page_table.reshape(-1), q_all, *([cache_k] * pps), *([cache_v] * pps), k_new, v_new, bias_t)


def _router_kernel(x_ref, g_ref, rw_ref, rb_ref, tril_ref, hn_ref, gate_ref, e_ref, rank_ref,
                   cnt_ref, run_ref):
    @pl.when(pl.program_id(0) == 0)
    def _():
        run_ref[...] = jnp.zeros(run_ref.shape, F32)

    x = x_ref[...]
    hn = x * lax.rsqrt(jnp.mean(x * x, axis=-1, keepdims=True) + RMS_EPS) * g_ref[...]
    hn_ref[...] = hn
    a1, a2, a3 = _split3(hn)
    b1, b2, b3 = _split3(rw_ref[...])
    logits = (_dot(a1, b1) + _dot(a1, b2) + _dot(a2, b1) + _dot(a2, b2) + _dot(a1, b3)
              + _dot(a3, b1)) + rb_ref[...]
    lane = lax.broadcasted_iota(I32, logits.shape, 1).astype(F32)
    onehot = jnp.zeros(logits.shape, F32)
    vals, idxs = [], []
    s = logits
    for _ in range(TOP_K):
        mx = jnp.max(s, axis=-1, keepdims=True)
        first = jnp.min(jnp.where(s == mx, lane, BIG_INDEX), axis=-1, keepdims=True)
        hit = lane == first
        onehot = jnp.where(hit, 1.0, onehot)
        s = jnp.where(hit, NEG, s)
        vals.append(mx)
        idxs.append(first)
    ex = [jnp.exp(v - vals[0]) for v in vals]
    den = ex[0] + ex[1] + ex[2] + ex[3]
    before = _dot(tril_ref[...], onehot.astype(BF16)) + run_ref[0:1, :]
    gate = jnp.zeros(logits.shape, F32)
    e_out = jnp.zeros(logits.shape, I32)
    rank = jnp.zeros(logits.shape, I32)
    for kk in range(TOP_K):
        slot = lane == kk
        rk = jnp.sum(jnp.where(lane == idxs[kk], before, 0.0), axis=-1, keepdims=True)
        gate = jnp.where(slot, ex[kk] / den, gate)
        e_out = jnp.where(slot, idxs[kk].astype(I32), e_out)
        rank = jnp.where(slot, rk.astype(I32), rank)
    gate_ref[...] = gate
    e_ref[...] = e_out
    rank_ref[...] = rank
    run_ref[...] = run_ref[...] + jnp.sum(onehot, axis=0, keepdims=True)
    cnt_ref[...] = run_ref[...]


def router(x, norm_g, router_w_pad, router_b_pad, tb):
    t, d = x.shape
    tril = (lax.broadcasted_iota(I32, (tb, tb), 0) > lax.broadcasted_iota(I32, (tb, tb), 1)).astype(BF16)
    row = lambda width: pl.BlockSpec((tb, width), lambda i: (i, 0))
    const = lambda shape: pl.BlockSpec(shape, lambda i: (0, 0))
    return pl.pallas_call(
        _router_kernel,
        out_shape=[jax.ShapeDtypeStruct((t, d), F32), jax.ShapeDtypeStruct((t, LANES), F32),
                   jax.ShapeDtypeStruct((t, LANES), I32), jax.ShapeDtypeStruct((t, LANES), I32),
                   jax.ShapeDtypeStruct((SUBLANES, LANES), F32)],
        grid=(t // tb,),
        in_specs=[row(d), const((1, d)), const((d, LANES)), const((1, LANES)), const((tb, tb))],
        out_specs=[row(d), row(LANES), row(LANES), row(LANES), const((SUBLANES, LANES))],
        scratch_shapes=[pltpu.VMEM((SUBLANES, LANES), F32)],
        compiler_params=pltpu.CompilerParams(
            dimension_semantics=("arbitrary",),
            vmem_limit_bytes=_vmem_limit(10 * _nbytes((tb, d), F32), 8 * _nbytes((d, LANES), F32))),
        name="router",
    )(x, norm_g.reshape(1, d), router_w_pad, router_b_pad, tril)


def _row_copy(src_hbm, buf, sem, src_row, dst_row):
    return pltpu.make_async_copy(src_hbm.at[pl.ds(src_row, 1), :], buf.at[pl.ds(dst_row, 1), :], sem)


def _gather_rows_kernel(idx_ref, nxt_ref, src_hbm, o_ref, buf, sem):
    i = pl.program_id(0)
    n = buf.shape[1]
    slot = i % 2

    def issue(ids_ref, s):
        def body(r, c):
            _row_copy(src_hbm, buf.at[s], sem.at[s], ids_ref[0, 0, r], r).start()
            return c
        lax.fori_loop(0, n, body, 0)

    @pl.when(i == 0)
    def _():
        issue(idx_ref, 0)

    @pl.when(i + 1 < pl.num_programs(0))
    def _():
        issue(nxt_ref, 1 - slot)

    def drain(r, c):
        _row_copy(src_hbm, buf.at[slot], sem.at[slot], 0, r).wait()
        return c

    lax.fori_loop(0, n, drain, 0)
    o_ref[...] = buf[slot].astype(o_ref.dtype)


def gather_rows(src, row_idx, tm, out_dtype):
    n_rows = row_idx.shape[0]
    d = src.shape[1]
    n_steps = n_rows // tm
    ids = row_idx.reshape(n_steps, 1, tm)
    return pl.pallas_call(
        _gather_rows_kernel,
        out_shape=jax.ShapeDtypeStruct((n_rows, d), out_dtype),
        grid=(n_steps,),
        in_specs=[pl.BlockSpec((1, 1, tm), lambda i: (i, 0, 0), memory_space=pltpu.SMEM),
                  pl.BlockSpec((1, 1, tm), lambda i: (jnp.minimum(i + 1, n_steps - 1), 0, 0),
                               memory_space=pltpu.SMEM),
                  pl.BlockSpec(memory_space=pl.ANY)],
        out_specs=pl.BlockSpec((tm, d), lambda i: (i, 0)),
        scratch_shapes=[pltpu.VMEM((2, tm, d), src.dtype), pltpu.SemaphoreType.DMA((2,))],
        compiler_params=pltpu.CompilerParams(
            dimension_semantics=("arbitrary",),
            vmem_limit_bytes=_vmem_limit(3 * _nbytes((tm, d), F32), 2 * _nbytes((tm, d), out_dtype))),
        name="gather_rows",
    )(ids, ids, src)


def _combine_kernel(idx_ref, y_hbm, x_ref, gate_ref, o_ref, buf, sem, *, tb):
    n = buf.shape[0]

    def issue(r, c):
        _row_copy(y_hbm, buf, sem, idx_ref[0, 0, r], r).start()
        return c

    def drain(r, c):
        _row_copy(y_hbm, buf, sem, 0, r).wait()
        return c

    lax.fori_loop(0, n, issue, 0)
    lax.fori_loop(0, n, drain, 0)
    gate = gate_ref[...]
    ffn = gate[:, 0:1] * buf[0:tb, :]
    for kk in range(1, TOP_K):
        ffn = ffn + gate[:, kk:kk + 1] * buf[kk * tb:(kk + 1) * tb, :]
    o_ref[...] = x_ref[...] + ffn


def moe_combine(yb, x, gate, dest, tb):
    t, d = x.shape
    idx = dest.reshape(t // tb, tb, TOP_K).transpose(0, 2, 1).reshape(t // tb, 1, TOP_K * tb)
    kern = functools.partial(_combine_kernel, tb=tb)
    return pl.pallas_call(
        kern,
        out_shape=jax.ShapeDtypeStruct((t, d), F32),
        grid=(t // tb,),
        in_specs=[pl.BlockSpec((1, 1, TOP_K * tb), lambda i: (i, 0, 0), memory_space=pltpu.SMEM),
                  pl.BlockSpec(memory_space=pl.ANY),
                  pl.BlockSpec((tb, d), lambda i: (i, 0)),
                  pl.BlockSpec((tb, LANES), lambda i: (i, 0))],
        out_specs=pl.BlockSpec((tb, d), lambda i: (i, 0)),
        scratch_shapes=[pltpu.VMEM((TOP_K * tb, d), F32), pltpu.SemaphoreType.DMA],
        compiler_params=pltpu.CompilerParams(
            dimension_semantics=("arbitrary",),
            vmem_limit_bytes=_vmem_limit((TOP_K + 6) * _nbytes((tb, d), F32))),
        name="moe_combine",
    )(idx, yb, x, gate)


def _first_of_expert(be_ref, i):
    return (i == 0) | (be_ref[i] != be_ref[jnp.maximum(i - 1, 0)])


def _moe_gu_kernel(be_ref, nu_ref, x_ref, wg_ref, wu_ref, bg_ref, bu_ref, o_ref, wgb_ref, wub_ref):
    i = pl.program_id(1)
    used = i < nu_ref[0]

    @pl.when(used & _first_of_expert(be_ref, i))
    def _():
        wgb_ref[...] = wg_ref[...].astype(BF16)
        wub_ref[...] = wu_ref[...].astype(BF16)

    @pl.when(used)
    def _():
        x = x_ref[...]
        g = jnp.minimum(_dot(x, wgb_ref[...]) + bg_ref[...], SWIGLU_LIMIT)
        u = jnp.clip(_dot(x, wub_ref[...]) + bu_ref[...], -SWIGLU_LIMIT, SWIGLU_LIMIT)
        o_ref[...] = ((u + 1.0) * (g * jax.nn.sigmoid(SWIGLU_ALPHA * g))).astype(o_ref.dtype)

    @pl.when(jnp.logical_not(used))
    def _():
        o_ref[...] = jnp.zeros(o_ref.shape, o_ref.dtype)


def moe_gate_up(xb, w_gu, b_gu, block_e, n_used, n_blocks):
    n_rows, d = xb.shape
    tm, tn = MOE_TM, MOE_TN_GU
    nj = D_FF // tn
    blk = lambda i, nu: jnp.minimum(i, nu[0] - 1)
    grid_spec = pltpu.PrefetchScalarGridSpec(
        num_scalar_prefetch=2,
        grid=(nj, n_blocks),
        in_specs=[
            pl.BlockSpec((tm, d), lambda j, i, be, nu: (blk(i, nu), 0)),
            pl.BlockSpec((None, d, tn), lambda j, i, be, nu: (be[blk(i, nu)], 0, j)),
            pl.BlockSpec((None, d, tn), lambda j, i, be, nu: (be[blk(i, nu)], 0, nj + j)),
            pl.BlockSpec((None, 1, tn), lambda j, i, be, nu: (be[blk(i, nu)], 0, j)),
            pl.BlockSpec((None, 1, tn), lambda j, i, be, nu: (be[blk(i, nu)], 0, nj + j)),
        ],
        out_specs=pl.BlockSpec((tm, tn), lambda j, i, be, nu: (i, j)),
        scratch_shapes=[pltpu.VMEM((d, tn), BF16), pltpu.VMEM((d, tn), BF16)],
    )
    return pl.pallas_call(
        _moe_gu_kernel,
        out_shape=jax.ShapeDtypeStruct((n_rows, D_FF), BF16),
        grid_spec=grid_spec,
        compiler_params=pltpu.CompilerParams(
            dimension_semantics=("arbitrary", "arbitrary"),
            vmem_limit_bytes=_vmem_limit(2 * _nbytes((tm, d), BF16), 4 * _nbytes((d, tn), F32),
                                         2 * _nbytes((d, tn), BF16), 8 * _nbytes((tm, tn), F32))),
        name="moe_gate_up",
    )(block_e, n_used, xb, w_gu, w_gu, b_gu.reshape(N_EXPERTS, 1, 2 * D_FF),
      b_gu.reshape(N_EXPERTS, 1, 2 * D_FF))


def _moe_down_kernel(be_ref, nu_ref, a_ref, w_ref, b_ref, o_ref, wb_ref):
    i = pl.program_id(1)
    used = i < nu_ref[0]

    @pl.when(used & _first_of_expert(be_ref, i))
    def _():
        wb_ref[...] = w_ref[...].astype(BF16)

    @pl.when(used)
    def _():
        o_ref[...] = _dot(a_ref[...], wb_ref[...]) + b_ref[...]

    @pl.when(jnp.logical_not(used))
    def _():
        o_ref[...] = jnp.zeros(o_ref.shape, o_ref.dtype)


def moe_down(act, w_down, b_down, block_e, n_used, n_blocks):
    n_rows, dff = act.shape
    tm, tn = MOE_TM, MOE_TN_DOWN
    blk = lambda i, nu: jnp.minimum(i, nu[0] - 1)
    grid_spec = pltpu.PrefetchScalarGridSpec(
        num_scalar_prefetch=2,
        grid=(D_MODEL // tn, n_blocks),
        in_specs=[
            pl.BlockSpec((tm, dff), lambda j, i, be, nu: (blk(i, nu), 0)),
            pl.BlockSpec((None, dff, tn), lambda j, i, be, nu: (be[blk(i, nu)], 0, j)),
            pl.BlockSpec((None, 1, tn), lambda j, i, be, nu: (be[blk(i, nu)], 0, j)),
        ],
        out_specs=pl.BlockSpec((tm, tn), lambda j, i, be, nu: (i, j)),
        scratch_shapes=[pltpu.VMEM((dff, tn), BF16)],
    )
    return pl.pallas_call(
        _moe_down_kernel,
        out_shape=jax.ShapeDtypeStruct((n_rows, D_MODEL), F32),
        grid_spec=grid_spec,
        compiler_params=pltpu.CompilerParams(
            dimension_semantics=("arbitrary", "arbitrary"),
            vmem_limit_bytes=_vmem_limit(2 * _nbytes((tm, dff), BF16), 2 * _nbytes((dff, tn), F32),
                                         _nbytes((dff, tn), BF16), 4 * _nbytes((tm, tn), F32))),
        name="moe_down",
    )(block_e, n_used, act, w_down, b_down.reshape(N_EXPERTS, 1, D_MODEL))


def _t5_bucket(dist):
    n = jnp.maximum(dist, 0)
    max_exact = N_BUCKETS // 2
    nf = jnp.maximum(n, 1).astype(F32)
    large = max_exact + (jnp.log(nf / max_exact) / math.log(MAX_DISTANCE / max_exact)
                         * (N_BUCKETS - max_exact)).astype(I32)
    return jnp.where(n < max_exact, n, jnp.minimum(large, N_BUCKETS - 1))


def _bias_lookup(rel_bias, dist):
    onehot = (_t5_bucket(dist)[..., None] == jnp.arange(N_BUCKETS, dtype=I32)).astype(F32)
    return jnp.einsum("...b,bh->...h", onehot, rel_bias.astype(F32), precision=lax.Precision.HIGHEST)


def _state_to_tiles(s):
    n = s.shape[0]
    s = s.reshape(n, RW_LANE_TILES, LANES // RW_HD, RW_HD, RW_HD).transpose(0, 1, 3, 2, 4)
    return s.reshape(n, RW_LANE_TILES, RW_HD, LANES)


def _tiles_to_state(s):
    n = s.shape[0]
    s = s.reshape(n, RW_LANE_TILES, RW_HD, LANES // RW_HD, RW_HD).transpose(0, 1, 3, 2, 4)
    return s.reshape(n, RW_HEADS, RW_HD, RW_HD)


def _pad_rows(w, row0, n_rows):
    return jnp.zeros((n_rows, w.shape[1]), BF16).at[row0:row0 + w.shape[0]].set(w.astype(BF16))


def kernel(x_prompt, x_sample, cache_k, cache_v, page_table, state_rwkv, state_shift, norm1_g, w_in,
           q_norm_g, k_norm_g, rel_bias, mu_shift, w0, w_lora_up, a0, a_lora_up, g_lora_up, k_k, k_a,
           r_k, lnx_w, lnx_b, w_branch_a, w_branch_b, w_out, norm2_g, router_w, router_b, w_gu, b_gu,
           w_down, b_down):
    n_p, t_p, d = x_prompt.shape
    n_s, t_s, _ = x_sample.shape
    tok_p, tok_s = n_p * t_p, n_s * t_s
    n_tok = tok_p + tok_s
    x = jnp.concatenate([x_prompt.reshape(tok_p, d), x_sample.reshape(tok_s, d)], axis=0)

    h = rmsnorm_rows(x, norm1_g, BF16, 256)
    cb = ATT_W // MM_TN
    p_q = matmul(h, w_in, ATT_W, col_block0=0, name="proj_q")
    p_k = matmul(h, w_in, ATT_W, col_block0=cb, name="proj_k")
    p_v = matmul(h, w_in, ATT_W, col_block0=2 * cb, name="proj_v")
    p_rw = matmul(h, w_in, RW_PAD, col_block0=3 * cb, name="proj_rw")
    w_gate = w_in[:, 3 * ATT_W + RW_COLS:]
    p_ga = matmul(h, w_gate, D_MODEL, col_block0=0, name="proj_gate_a")
    p_gb = matmul(h, w_gate, D_MODEL, col_block0=D_MODEL // MM_TN, name="proj_gate_b")

    head_rows = n_tok * ATT_HEADS
    q_n = rmsnorm_rows(p_q.reshape(head_rows, ATT_HD), q_norm_g, BF16, 2048).reshape(n_tok, ATT_W)
    k_n = rmsnorm_rows(p_k.reshape(head_rows, ATT_HD), k_norm_g, F32, 2048).reshape(n_tok, ATT_W)

    past = page_table.shape[1] * cache_k.shape[1]
    n_bias = 6
    ri = jnp.arange(MOBA_BLOCK, dtype=I32)
    dist = (jnp.arange(n_bias, dtype=I32)[:, None, None] * MOBA_BLOCK + ri[None, :, None] - ri[None, None, :])
    bias_tiles = _bias_lookup(rel_bias, dist).transpose(3, 0, 1, 2)
    o_att_p = moba_prompt(q_n, k_n, p_v, bias_tiles, n_p, t_p)

    q_s = q_n[tok_p:].reshape(n_s, t_s, ATT_HEADS, ATT_HD)
    q_all = q_s.transpose(0, 3, 2, 1).reshape(n_s, ATT_HD, ATT_HEADS * t_s)
    new_rows = lambda a: a[tok_p:].reshape(n_s, t_s * ATT_HEADS, ATT_HD)
    key_pos = jnp.arange(past + LANES, dtype=I32)
    q_idx = jnp.arange(t_s, dtype=I32)
    d_s = past + q_idx[None, :] - key_pos[:, None]
    bias_t = _bias_lookup(rel_bias, d_s).transpose(0, 2, 1).reshape(past + LANES, ATT_HEADS * t_s)
    pool = cache_k.shape[0]
    o_att_s = moba_sample(q_all, cache_k.reshape(pool, -1, ATT_HD), cache_v.reshape(pool, -1, ATT_HD),
                          new_rows(k_n), new_rows(p_v), page_table, bias_t)
    o_att = jnp.concatenate([o_att_p, o_att_s.reshape(tok_s, ATT_W).astype(BF16)], axis=0)

    shift_s = jnp.pad(state_shift.astype(F32), ((0, 0), (0, RW_PAD - RW_COLS)))
    row = lambda a: a.reshape(1, -1).astype(F32)
    vecs = (jnp.pad(row(mu_shift), ((0, 0), (0, RW_PAD - RW_COLS))), row(w0), row(a0), row(k_k), row(k_a),
            row(r_k))
    loras = (_pad_rows(w_lora_up, 0, LORA_PAD), _pad_rows(a_lora_up, W_LORA, LORA_PAD),
             _pad_rows(g_lora_up, W_LORA + A_LORA, LORA_PAD))
    lane_i = jnp.arange(LANES, dtype=I32)
    bd = (lane_i[:, None] // RW_HD == lane_i[None, :] // RW_HD).astype(BF16)
    ipat = (jnp.arange(RW_HD, dtype=I32)[:, None] == lane_i[None, :] % RW_HD).astype(F32)
    def rwkv_group(row0, n, t, shift, s0, n_group, n_tok):
        prep = rwkv_prep(p_rw, row0, n, t, shift, vecs, loras, bd)
        tiles = lambda a: a.reshape(n, t, RW_LANE_TILES, LANES)
        y, s = rwkv_scan([tiles(a) for a in prep[:8]], s0, bd, ipat, n_group, n_tok)
        o = rwkv_post(y.reshape(n * t, RW_W), prep[8], prep[9], row(lnx_w), row(lnx_b), bd, 256)
        return o, s

    o_rw_p, s_p = rwkv_group(0, n_p, t_p, jnp.zeros((n_p, RW_PAD), F32),
                             jnp.zeros((n_p, RW_LANE_TILES, RW_HD, LANES), F32), n_p, 64)
    o_rw_s, s_s = rwkv_group(tok_p, n_s, t_s, shift_s, _state_to_tiles(state_rwkv.astype(F32)), 4, t_s)
    o_rw = jnp.concatenate([o_rw_p, o_rw_s], axis=0)

    m_a = matmul(o_att, w_branch_a, D_MODEL, epilogue="gate", gate=p_ga, name="branch_a")
    merged = matmul(o_rw, w_branch_b, D_MODEL, epilogue="gate", gate=p_gb, addend=m_a,
                    out_dtype=BF16, name="branch_b")
    x1 = matmul(merged, w_out, D_MODEL, epilogue="residual", addend=x, name="out_proj")

    rw_pad = jnp.pad(router_w.astype(F32), ((0, 0), (0, LANES - N_EXPERTS)))
    rb_pad = jnp.pad(router_b.astype(F32).reshape(1, -1), ((0, 0), (0, LANES - N_EXPERTS)),
                     constant_values=NEG)
    hn, gate, e_sel, rank, counts = router(x1, norm2_g, rw_pad, rb_pad, 256)
    tm = MOE_TM
    n_assign = n_tok * TOP_K
    n_blocks = -(-(n_assign + N_EXPERTS * (tm - 1)) // tm)
    counts = counts[0, :N_EXPERTS].astype(I32)
    padded = (counts + tm - 1) // tm * tm
    pad_end = jnp.cumsum(padded)
    pad_start = pad_end - padded
    dest = pad_start[e_sel[:, :TOP_K]] + rank[:, :TOP_K]
    row_tok = jnp.zeros((n_blocks * tm,), I32).at[dest.reshape(-1)].set(
        jnp.arange(n_assign, dtype=I32) // TOP_K)
    block_e = jnp.minimum(jnp.searchsorted(pad_end, jnp.arange(n_blocks, dtype=I32) * tm, side="right"),
                          N_EXPERTS - 1).astype(I32)
    n_used = (pad_end[-1:] // tm).astype(I32)
    xb = gather_rows(hn, row_tok, tm, BF16)
    act = moe_gate_up(xb, w_gu, b_gu, block_e, n_used, n_blocks)
    yb = moe_down(act, w_down, b_down, block_e, n_used, n_blocks)
    y_out = moe_combine(yb, x1, gate, dest, 128)

    lead = lambda a, lo, n, t: a[lo:lo + n * t].reshape(n, t, ATT_HEADS, ATT_HD)
    last_rw = lambda lo, n, t: p_rw[lo + t - 1:lo + n * t:t, :RW_COLS]
    return (y_out[:tok_p].reshape(n_p, t_p, d), y_out[tok_p:].reshape(n_s, t_s, d),
            lead(k_n, 0, n_p, t_p), lead(p_v, 0, n_p, t_p),
            lead(k_n, tok_p, n_s, t_s), lead(p_v, tok_p, n_s, t_s),
            _tiles_to_state(s_p), last_rw(0, n_p, t_p),
            _tiles_to_state(s_s), last_rw(tok_p, n_s, t_s))
```

```python
import functools
import math

import jax
import jax.numpy as jnp
from jax import lax
from jax.experimental import pallas as pl
from jax.experimental.pallas import tpu as pltpu

F32 = jnp.float32
BF16 = jnp.bfloat16
I32 = jnp.int32

LANES = 128
SUBLANES = 8
VMEM_CAP_BYTES = 60 * 1024 * 1024
VMEM_SLACK_BYTES = 6 * 1024 * 1024

D_MODEL = 4096
ATT_HD = 128
ATT_W = D_MODEL // 2
ATT_HEADS = ATT_W // ATT_HD
MOBA_BLOCK = 256
MOBA_TOPK = 3
ATT_SCALE = ATT_HD ** -0.5
N_BUCKETS = 32
MAX_DISTANCE = 1024
RW_HD = 64
RW_W = D_MODEL // 2
RW_HEADS = RW_W // RW_HD
W_LORA = max(32, int(round(RW_W ** 0.5 * 1.8 / 32)) * 32)
A_LORA = max(32, int(round(RW_W ** 0.5 * 1.8 / 32)) * 32)
G_LORA = max(32, int(round(RW_W ** 0.8 * 0.6 / 32)) * 32)
RW_COLS = 3 * RW_W + W_LORA + A_LORA + G_LORA
LORA_COLS = W_LORA + A_LORA + G_LORA
LNX_EPS = 64e-5
N_EXPERTS = 32
TOP_K = 4
D_FF = D_MODEL
SWIGLU_LIMIT = 7.0
SWIGLU_ALPHA = 1.702
RMS_EPS = 1e-6
NEG = -1e30
BIG_INDEX = 1e9

RW_LANE_TILES = RW_W // LANES
MM_TM = 512
MM_TN = 512
RW_PAD = -(-RW_COLS // MM_TN) * MM_TN
LORA_PAD = RW_PAD - 3 * RW_W
MOE_TM = 512
MOE_TN_GU = 512
MOE_TN_DOWN = 512
MOBA_GROUP = 4
SAMPLE_PAGES_PER_STEP = 8
PREP_TB = 64


def _vmem_limit(*buffer_bytes):
    return int(min(VMEM_CAP_BYTES, sum(buffer_bytes) + VMEM_SLACK_BYTES))


def _nbytes(shape, dtype):
    return math.prod(shape) * jnp.dtype(dtype).itemsize


def _split3(x):
    x1 = x.astype(BF16)
    r1 = x - x1.astype(F32)
    x2 = r1.astype(BF16)
    x3 = (r1 - x2.astype(F32)).astype(BF16)
    return x1, x2, x3


def _dot(a, b):
    return jnp.dot(a, b, preferred_element_type=F32)


def _rmsnorm_kernel(x_ref, g_ref, o_ref):
    x = x_ref[...].astype(F32)
    y = x * lax.rsqrt(jnp.mean(x * x, axis=-1, keepdims=True) + RMS_EPS)
    o_ref[...] = (y * g_ref[...]).astype(o_ref.dtype)


def rmsnorm_rows(x, g, out_dtype, tb):
    m, d = x.shape
    return pl.pallas_call(
        _rmsnorm_kernel,
        out_shape=jax.ShapeDtypeStruct((m, d), out_dtype),
        grid=(m // tb,),
        in_specs=[pl.BlockSpec((tb, d), lambda i: (i, 0)),
                  pl.BlockSpec((1, d), lambda i: (0, 0))],
        out_specs=pl.BlockSpec((tb, d), lambda i: (i, 0)),
        compiler_params=pltpu.CompilerParams(
            dimension_semantics=("arbitrary",),
            vmem_limit_bytes=_vmem_limit(4 * _nbytes((tb, d), F32))),
        name="rmsnorm_rows",
    )(x, g.reshape(1, d).astype(F32))


def _matmul_kernel(*refs, epilogue, has_addend):
    a_ref, w_ref = refs[0], refs[1]
    o_ref, wb_ref = refs[-2], refs[-1]

    @pl.when(pl.program_id(1) == 0)
    def _():
        wb_ref[...] = w_ref[...].astype(BF16)

    acc = _dot(a_ref[...], wb_ref[...])
    if epilogue == "gate":
        acc = jax.nn.sigmoid(refs[2][...]) * acc
        if has_addend:
            acc = acc + refs[3][...]
    elif epilogue == "residual":
        acc = refs[2][...] + acc
    o_ref[...] = acc.astype(o_ref.dtype)


def matmul(a, w, n_out, *, col_block0=0, epilogue=None, gate=None, addend=None,
           out_dtype=F32, name="matmul"):
    m, k = a.shape
    tm, tn = MM_TM, MM_TN
    extra, extra_specs = [], []
    io_spec = pl.BlockSpec((tm, tn), lambda j, i: (i, j))
    if epilogue == "gate":
        extra.append(gate)
        extra_specs.append(io_spec)
        if addend is not None:
            extra.append(addend)
            extra_specs.append(io_spec)
    elif epilogue == "residual":
        extra.append(addend)
        extra_specs.append(io_spec)
    kern = functools.partial(_matmul_kernel, epilogue=epilogue,
                             has_addend=(epilogue == "gate" and addend is not None))
    return pl.pallas_call(
        kern,
        out_shape=jax.ShapeDtypeStruct((m, n_out), out_dtype),
        grid=(n_out // tn, m // tm),
        in_specs=[pl.BlockSpec((tm, k), lambda j, i: (i, 0)),
                  pl.BlockSpec((k, tn), lambda j, i: (0, j + col_block0))] + extra_specs,
        out_specs=io_spec,
        scratch_shapes=[pltpu.VMEM((k, tn), BF16)],
        compiler_params=pltpu.CompilerParams(
            dimension_semantics=("arbitrary", "arbitrary"),
            vmem_limit_bytes=_vmem_limit(
                2 * _nbytes((tm, k), BF16), 2 * _nbytes((k, tn), F32), _nbytes((k, tn), BF16),
                2 * (2 + len(extra)) * _nbytes((tm, tn), F32))),
        name=name,
    )(a, w, *extra)


def _segsum(x, bd):
    outs = []
    for c in range(x.shape[-1] // LANES):
        x1, x2, x3 = _split3(x[:, c * LANES:(c + 1) * LANES])
        outs.append(_dot(x1, bd) + _dot(x2, bd) + _dot(x3, bd))
    return jnp.concatenate(outs, axis=-1)


def _softplus(z):
    return jnp.maximum(z, 0.0) + jnp.log(1.0 + jnp.exp(-jnp.abs(z)))


def _rwkv_prep_kernel(rw_ref, shift_ref, mu_ref, w0_ref, a0_ref, kk_g_ref, ka_ref, rk_ref,
                      wl_ref, al_ref, gl_ref, bd_ref,
                      w_o, r_o, a_o, b_o, kx_o, v_o, g_o, bonus_o, carry_ref, *, seq_len):
    rw = rw_ref[...]
    tb = rw.shape[0]
    row = lax.broadcasted_iota(I32, rw.shape, 0)
    prev = pltpu.roll(rw, 1, axis=0)
    if seq_len >= tb:
        @pl.when(pl.program_id(0) == 0)
        def _():
            carry_ref[...] = jnp.zeros(carry_ref.shape, F32)

        starts = pl.program_id(0) % (seq_len // tb) == 0
        first = jnp.where(starts, shift_ref[...], carry_ref[...])
        prev = jnp.where(row == 0, first, prev)
        carry_ref[...] = rw[tb - 1:tb, :]
    else:
        for s in range(tb // seq_len):
            prev = jnp.where(row == s * seq_len, shift_ref[s:s + 1, :], prev)
    m = rw + (prev - rw) * mu_ref[...]
    r = m[:, 0:RW_W]
    k = m[:, RW_W:2 * RW_W]
    v = m[:, 2 * RW_W:3 * RW_W]
    x = m[:, 3 * RW_W:]
    bd = bd_ref[...]
    u = w0_ref[...] + _dot(jnp.tanh(x).astype(BF16), wl_ref[...])
    w_log = -_softplus(-u) - 0.5
    decay = jnp.exp(-jnp.exp(w_log))
    a = jax.nn.sigmoid(a0_ref[...] + _dot(x.astype(BF16), al_ref[...]))
    g = _dot(jax.nn.sigmoid(x).astype(BF16), gl_ref[...])
    kk = k * kk_g_ref[...]
    kk = kk / jnp.maximum(jnp.sqrt(_segsum(kk * kk, bd)), 1e-12)
    kx = k * (1.0 + (a - 1.0) * ka_ref[...])
    bvec = kk * a
    w_o[...] = decay
    r_o[...] = r
    a_o[...] = -kk
    b_o[...] = bvec
    kx_o[...] = kx
    v_o[...] = v
    g_o[...] = g
    bonus_o[...] = _segsum(r * kx * rk_ref[...], bd) * v


def rwkv_prep(rw, row0, n_seq, seq_len, shift, vecs, loras, bd):
    tb = PREP_TB
    t = n_seq * seq_len
    blk0 = row0 // tb
    const = lambda shape: pl.BlockSpec(shape, lambda i: (0, 0))
    if seq_len >= tb:
        per_seq = seq_len // tb
        shift = shift.reshape(n_seq, 1, RW_PAD)
        shift_spec = pl.BlockSpec((None, 1, RW_PAD), lambda i: (i // per_seq, 0, 0))
    else:
        shift_spec = pl.BlockSpec((tb // seq_len, RW_PAD), lambda i: (i, 0))
    out_spec = pl.BlockSpec((tb, RW_W), lambda i: (i, 0))
    mu, w0, a0, k_k, k_a, r_k = vecs
    n_out = 8
    return pl.pallas_call(
        functools.partial(_rwkv_prep_kernel, seq_len=seq_len),
        out_shape=[jax.ShapeDtypeStruct((t, RW_W), F32)] * n_out,
        grid=(t // tb,),
        in_specs=[pl.BlockSpec((tb, RW_PAD), lambda i: (i + blk0, 0)), shift_spec, const((1, RW_PAD))]
                 + [const((1, RW_W))] * 5 + [const((LORA_PAD, RW_W))] * 3 + [const((LANES, LANES))],
        out_specs=[out_spec] * n_out,
        scratch_shapes=[pltpu.VMEM((1, RW_PAD), F32)],
        compiler_params=pltpu.CompilerParams(
            dimension_semantics=("arbitrary",),
            vmem_limit_bytes=_vmem_limit(6 * _nbytes((tb, RW_PAD), F32),
                                         2 * n_out * _nbytes((tb, RW_W), F32),
                                         6 * _nbytes((LORA_PAD, RW_W), BF16),
                                         16 * _nbytes((tb, RW_W), F32))),
        name="rwkv_prep",
    )(rw, shift, mu, w0, a0, k_k, k_a, r_k, *loras, bd)


def _rowmul_bf16(xb, row):
    pack = 2 * SUBLANES
    rb = jnp.broadcast_to(row, (pack, LANES)).astype(BF16)
    return (xb.reshape(xb.shape[0] // pack, pack, LANES) * rb[None]).reshape(xb.shape)


def _rwkv_scan_kernel(w_ref, r_ref, a_ref, b_ref, kx_ref, v_ref, s0_ref, bd_ref, ipat_ref,
                      y_ref, s_ref, lhs_ref, res_ref, *, n_group, n_tok):
    @pl.when(pl.program_id(1) == 0)
    def _():
        s_ref[...] = s0_ref[...]

    ipat = ipat_ref[...]
    ipat_b = ipat.astype(BF16)
    seg = 3 * RW_HD
    blocks = [(g, c) for g in range(n_group) for c in range(RW_LANE_TILES)]

    def put_y(g, t_out, c, spread):
        y_ref[g, t_out, pl.ds(c, 1), :] = jnp.sum(spread * ipat, axis=0, keepdims=True)

    def step(t, carry):
        tp = jnp.maximum(t - 1, 0)
        for g in range(n_group):
            a, rp, vv = a_ref[g, t], r_ref[g, tp], v_ref[g, t]
            for c in range(RW_LANE_TILES):
                sb = s_ref[g, c].astype(BF16)
                base = (g * RW_LANE_TILES + c) * seg
                lhs_ref[base:base + RW_HD, :] = _rowmul_bf16(sb, a[c:c + 1])
                lhs_ref[base + RW_HD:base + 2 * RW_HD, :] = _rowmul_bf16(sb, rp[c:c + 1])
                lhs_ref[base + 2 * RW_HD:base + seg, :] = _rowmul_bf16(ipat_b, vv[c:c + 1])
        res_ref[...] = _dot(lhs_ref[...], bd_ref[...])
        for g in range(n_group):
            w, b, kx = w_ref[g, t], b_ref[g, t], kx_ref[g, t]
            for c in range(RW_LANE_TILES):
                base = (g * RW_LANE_TILES + c) * seg
                sa = res_ref[base:base + RW_HD, :]
                vx = res_ref[base + 2 * RW_HD:base + seg, :]
                put_y(g, tp, c, res_ref[base + RW_HD:base + 2 * RW_HD, :])
                s_ref[g, c] = s_ref[g, c] * w[c:c + 1] + sa * b[c:c + 1] + vx * kx[c:c + 1]
        return carry

    lax.fori_loop(0, n_tok, step, 0)

    n_rows = len(blocks) * RW_HD
    for g, c in blocks:
        base = (g * RW_LANE_TILES + c) * RW_HD
        lhs_ref[base:base + RW_HD, :] = _rowmul_bf16(s_ref[g, c].astype(BF16), r_ref[g, n_tok - 1][c:c + 1])
    res_ref[0:n_rows, :] = _dot(lhs_ref[0:n_rows, :], bd_ref[...])
    for g, c in blocks:
        base = (g * RW_LANE_TILES + c) * RW_HD
        put_y(g, n_tok - 1, c, res_ref[base:base + RW_HD, :])


def rwkv_scan(seq_inputs, s0, bd, ipat, n_group, n_tok):
    n, t = seq_inputs[0].shape[:2]
    tok = pl.BlockSpec((n_group, n_tok, RW_LANE_TILES, LANES), lambda i, j: (i, j, 0, 0))
    st = pl.BlockSpec((n_group, RW_LANE_TILES, RW_HD, LANES), lambda i, j: (i, 0, 0, 0))
    lhs_rows = n_group * RW_LANE_TILES * 3 * RW_HD
    kern = functools.partial(_rwkv_scan_kernel, n_group=n_group, n_tok=n_tok)
    tok_bytes = _nbytes((n_group, n_tok, RW_LANE_TILES, LANES), F32)
    st_bytes = _nbytes((n_group, RW_LANE_TILES, RW_HD, LANES), F32)
    return pl.pallas_call(
        kern,
        out_shape=[jax.ShapeDtypeStruct((n, t, RW_LANE_TILES, LANES), F32),
                   jax.ShapeDtypeStruct(s0.shape, F32)],
        grid=(n // n_group, t // n_tok),
        in_specs=[tok] * 6 + [st, pl.BlockSpec((LANES, LANES), lambda i, j: (0, 0)),
                              pl.BlockSpec((RW_HD, LANES), lambda i, j: (0, 0))],
        out_specs=[tok, st],
        scratch_shapes=[pltpu.VMEM((lhs_rows, LANES), BF16), pltpu.VMEM((lhs_rows, LANES), F32)],
        compiler_params=pltpu.CompilerParams(
            dimension_semantics=("arbitrary", "arbitrary"),
            vmem_limit_bytes=_vmem_limit(14 * tok_bytes, 4 * st_bytes,
                                         _nbytes((lhs_rows, LANES), BF16),
                                         _nbytes((lhs_rows, LANES), F32))),
        name="rwkv_scan",
    )(*seq_inputs, s0, bd, ipat)


def _rwkv_post_kernel(y_ref, g_ref, bonus_ref, lw_ref, lb_ref, bd_ref, o_ref):
    y = y_ref[...]
    bd = bd_ref[...]
    mu = _segsum(y, bd) * (1.0 / RW_HD)
    d = y - mu
    var = _segsum(d * d, bd) * (1.0 / RW_HD)
    yn = d * lax.rsqrt(var + LNX_EPS) * lw_ref[...] + lb_ref[...]
    o_ref[...] = ((yn + bonus_ref[...]) * g_ref[...]).astype(o_ref.dtype)


def rwkv_post(y, g, bonus, lnx_w, lnx_b, bd, tb):
    t = y.shape[0]
    row = pl.BlockSpec((tb, RW_W), lambda i: (i, 0))
    const = lambda shape: pl.BlockSpec(shape, lambda i: (0, 0))
    return pl.pallas_call(
        _rwkv_post_kernel,
        out_shape=jax.ShapeDtypeStruct((t, RW_W), BF16),
        grid=(t // tb,),
        in_specs=[row, row, row, const((1, RW_W)), const((1, RW_W)), const((LANES, LANES))],
        out_specs=row,
        compiler_params=pltpu.CompilerParams(
            dimension_semantics=("arbitrary",),
            vmem_limit_bytes=_vmem_limit(16 * _nbytes((tb, RW_W), F32))),
        name="rwkv_post",
    )(y, g, bonus, lnx_w, lnx_b, bd)


def _top_mask(score, index, n_pick, axis):
    index = index.astype(F32)
    sel = jnp.zeros(score.shape, F32)
    for _ in range(n_pick):
        mx = jnp.max(score, axis=axis, keepdims=True)
        first = jnp.min(jnp.where(score == mx, index, BIG_INDEX), axis=axis, keepdims=True)
        hit = index == first
        sel = jnp.maximum(sel, jnp.where(hit, jnp.where(mx > 0.5 * NEG, 1.0, 0.0), 0.0))
        score = jnp.where(hit, NEG, score)
    return sel


def _moba_prompt_kernel(q_ref, k_ref, v_ref, bias_ref, expand_ref, o_ref, kmean_ref, m_ref, l_ref,
                        acc_ref, *, n_blocks, n_bias):
    i = pl.program_id(2)
    bs = MOBA_BLOCK
    nt = (((1,), (1,)), ((), ()))

    @pl.when(i == 0)
    def _():
        kmean_ref[...] = jnp.zeros(kmean_ref.shape, F32)
        kmean_ref[0:n_blocks, :] = jnp.mean(k_ref[...].reshape(n_blocks, bs, ATT_HD), axis=1)

    q = q_ref[...]
    km1, km2, km3 = _split3(kmean_ref[...])
    score = (lax.dot_general(q, km1, nt, preferred_element_type=F32)
             + lax.dot_general(q, km2, nt, preferred_element_type=F32)
             + lax.dot_general(q, km3, nt, preferred_element_type=F32))
    lane = lax.broadcasted_iota(I32, score.shape, 1)
    sel = _top_mask(jnp.where(lane < i, score, NEG), lane, MOBA_TOPK, axis=1)
    attend = jnp.where(lane == i, 1.0, sel).astype(BF16)

    m_ref[...] = jnp.full(m_ref.shape, NEG, F32)
    l_ref[...] = jnp.zeros(l_ref.shape, F32)
    acc_ref[...] = jnp.zeros(acc_ref.shape, F32)
    gw = MOBA_GROUP * bs

    for g in range(n_blocks // MOBA_GROUP):
        @pl.when(g * MOBA_GROUP <= i)
        def _(g=g):
            rows = slice(g * gw, (g + 1) * gw)
            s = lax.dot_general(q, k_ref[rows, :].astype(BF16), nt, preferred_element_type=F32) * ATT_SCALE
            bias = [bias_ref[jnp.clip(i - (g * MOBA_GROUP + u), 0, n_bias - 1)] for u in range(MOBA_GROUP)]
            on = _dot(attend, expand_ref[g])
            s = jnp.where(on > 0.5, s + jnp.concatenate(bias, axis=-1), NEG)
            m_old = m_ref[...]
            m_new = jnp.maximum(m_old, jnp.max(s, axis=-1, keepdims=True))
            alpha = jnp.exp(m_old - m_new)
            p = jnp.exp(s - m_new)
            l_ref[...] = alpha * l_ref[...] + jnp.sum(p, axis=-1, keepdims=True)
            acc_ref[...] = alpha * acc_ref[...] + _dot(p.astype(BF16), v_ref[rows, :].astype(BF16))
            m_ref[...] = m_new

    o_ref[...] = (acc_ref[...] / l_ref[...]).astype(o_ref.dtype)


def moba_prompt(q, k, v, bias_tiles, n, t):
    bs = MOBA_BLOCK
    n_blocks = t // bs
    n_groups = n_blocks // MOBA_GROUP
    n_bias = bias_tiles.shape[1]
    gw = MOBA_GROUP * bs
    blk_of_key = jnp.arange(n_groups * gw, dtype=I32).reshape(n_groups, 1, gw) // bs
    expand = (jnp.arange(LANES, dtype=I32)[None, :, None] == blk_of_key).astype(BF16)
    qo = pl.BlockSpec((bs, ATT_HD), lambda s, h, i: (s * n_blocks + i, h))
    kv = pl.BlockSpec((t, ATT_HD), lambda s, h, i: (s, h))
    kern = functools.partial(_moba_prompt_kernel, n_blocks=n_blocks, n_bias=n_bias)
    return pl.pallas_call(
        kern,
        out_shape=jax.ShapeDtypeStruct((n * t, ATT_W), BF16),
        grid=(n, ATT_HEADS, n_blocks),
        in_specs=[qo, kv, kv,
                  pl.BlockSpec((None, n_bias, bs, bs), lambda s, h, i: (h, 0, 0, 0)),
                  pl.BlockSpec((n_groups, LANES, gw), lambda s, h, i: (0, 0, 0))],
        out_specs=qo,
        scratch_shapes=[pltpu.VMEM((LANES, ATT_HD), F32), pltpu.VMEM((bs, 1), F32),
                        pltpu.VMEM((bs, 1), F32), pltpu.VMEM((bs, ATT_HD), F32)],
        compiler_params=pltpu.CompilerParams(
            dimension_semantics=("arbitrary", "arbitrary", "arbitrary"),
            vmem_limit_bytes=_vmem_limit(4 * _nbytes((t, ATT_HD), F32),
                                         2 * _nbytes((n_bias, bs, bs), F32),
                                         2 * _nbytes((n_groups, LANES, gw), BF16),
                                         12 * _nbytes((bs, gw), F32))),
        name="moba_prompt",
    )(q, k, v, bias_tiles, expand)


def _moba_sample_kernel(pt_ref, qall_ref, *refs, n_pages, pps, n_new):
    del pt_ref
    kp_refs, vp_refs = refs[:pps], refs[pps:2 * pps]
    kn_ref, vn_ref, bias_ref, o_ref, r_ref, lg_ref, p_ref, acc_ref = refs[2 * pps:]
    j = pl.program_id(1)
    nh = ATT_HEADS
    page = kp_refs[0].shape[0] // nh
    past = n_pages * page
    n_ksteps = n_pages // pps
    n_blocks = past // MOBA_BLOCK
    rows_all = lg_ref.shape[0]
    lane = lax.broadcasted_iota(I32, (1, LANES), 1)
    head_of_lane = lane >> (n_new.bit_length() - 1)
    head_cols = [jnp.where(head_of_lane == h, 1.0, 0.0) for h in range(nh)]

    def raw_logits(src_ref, n_tok):
        r_ref[0:n_tok * nh, :] = _dot(src_ref[...].astype(BF16), qall_ref[...])
        out = r_ref[pl.ds(0, n_tok, stride=nh), :] * head_cols[0]
        for h in range(1, nh):
            out = out + r_ref[pl.ds(h, n_tok, stride=nh), :] * head_cols[h]
        return out

    @pl.when(j < n_ksteps)
    def _():
        for u in range(pps):
            row0 = pl.multiple_of((j * pps + u) * page, page)
            lg_ref[pl.ds(row0, page), :] = raw_logits(kp_refs[u], page)

    @pl.when(j == n_ksteps - 1)
    def _():
        new = raw_logits(kn_ref, n_new) * ATT_SCALE + bias_ref[past:past + n_new, :]
        jn = lax.broadcasted_iota(I32, new.shape, 0)
        qi = lax.broadcasted_iota(I32, new.shape, 1) & (n_new - 1)
        score = jnp.mean(lg_ref[0:past, :].reshape(n_blocks, MOBA_BLOCK, LANES), axis=1)
        sub = lax.broadcasted_iota(I32, score.shape, 0)
        sel = _top_mask(score, sub, MOBA_TOPK, axis=0)
        for kb in range(n_blocks):
            rows = slice(kb * MOBA_BLOCK, (kb + 1) * MOBA_BLOCK)
            s = lg_ref[rows, :] * ATT_SCALE + bias_ref[rows, :]
            lg_ref[rows, :] = jnp.where(sel[kb:kb + 1, :] > 0.5, s, NEG)
        lg_ref[past:past + n_new, :] = jnp.where(jn <= qi, new, NEG)
        lg_ref[past + n_new:rows_all, :] = jnp.full((rows_all - past - n_new, LANES), NEG, F32)
        lg = lg_ref[...]
        p = jnp.exp(lg - jnp.max(lg, axis=0, keepdims=True))
        p_ref[...] = p / jnp.sum(p, axis=0, keepdims=True)

    @pl.when(j == n_ksteps)
    def _():
        acc_ref[...] = jnp.zeros(acc_ref.shape, F32)

    def add_pv(p_rows, v_of_head):
        pt = p_rows.T.astype(BF16)
        for h in range(nh):
            lo = (h // 2) * 2 * n_new
            o = _dot(pt[lo:lo + 2 * n_new, :], v_of_head(h))
            off = (h % 2) * n_new
            acc_ref[h * n_new:(h + 1) * n_new, :] += o[off:off + n_new, :]

    @pl.when(j >= n_ksteps)
    def _():
        for u in range(pps):
            row0 = pl.multiple_of(((j - n_ksteps) * pps + u) * page, page)
            add_pv(p_ref[pl.ds(row0, page), :],
                   lambda h, u=u: vp_refs[u][pl.ds(h, page, stride=nh), :].astype(BF16))

    @pl.when(j == 2 * n_ksteps - 1)
    def _():
        def v_new(h):
            rows = vn_ref[pl.ds(h, n_new, stride=nh), :]
            return jnp.concatenate([rows, jnp.zeros((rows_all - past - n_new, ATT_HD), F32)],
                                   axis=0).astype(BF16)

        add_pv(p_ref[past:rows_all, :], v_new)
        acc = acc_ref[...]
        for h in range(nh):
            o_ref[:, h * ATT_HD:(h + 1) * ATT_HD] = acc[h * n_new:(h + 1) * n_new, :]


def moba_sample(q_all, cache_k, cache_v, k_new, v_new, page_table, bias_t):
    n_seq, n_pages = page_table.shape
    pps = SAMPLE_PAGES_PER_STEP
    page_rows = cache_k.shape[1]
    n_new = LANES // ATT_HEADS
    past = n_pages * page_rows // ATT_HEADS
    rows_all = past + LANES
    n_ksteps = n_pages // pps
    kern = functools.partial(_moba_sample_kernel, n_pages=n_pages, pps=pps, n_new=n_new)

    def page_spec(u, for_v):
        def index(s, j, pt):
            step = jnp.maximum(j - n_ksteps, 0) if for_v else jnp.minimum(j, n_ksteps - 1)
            return (pt[s * n_pages + step * pps + u], 0, 0)
        return pl.BlockSpec((None, page_rows, ATT_HD), index)

    per_seq = lambda rows, cols: pl.BlockSpec((None, rows, cols), lambda s, j, pt: (s, 0, 0))
    grid_spec = pltpu.PrefetchScalarGridSpec(
        num_scalar_prefetch=1,
        grid=(n_seq, 2 * n_ksteps),
        in_specs=[per_seq(ATT_HD, LANES)]
                 + [page_spec(u, False) for u in range(pps)] + [page_spec(u, True) for u in range(pps)]
                 + [per_seq(n_new * ATT_HEADS, ATT_HD), per_seq(n_new * ATT_HEADS, ATT_HD),
                    pl.BlockSpec((rows_all, LANES), lambda s, j, pt: (0, 0))],
        out_specs=per_seq(n_new, ATT_W),
        scratch_shapes=[pltpu.VMEM((page_rows, LANES), F32), pltpu.VMEM((rows_all, LANES), F32),
                        pltpu.VMEM((rows_all, LANES), F32), pltpu.VMEM((LANES, ATT_HD), F32)],
    )
    return pl.pallas_call(
        kern,
        out_shape=jax.ShapeDtypeStruct((n_seq, n_new, ATT_W), F32),
        grid_spec=grid_spec,
        compiler_params=pltpu.CompilerParams(
            dimension_semantics=("arbitrary", "arbitrary"),
            vmem_limit_bytes=_vmem_limit((4 * pps + 4) * _nbytes((page_rows, ATT_HD), F32),
                                         8 * _nbytes((rows_all, LANES), F32))),
        name="moba_sample",
    )(page_table.reshape(-1), q_all, *([cache_k] * pps), *([cache_v] * pps), k_new, v_new, bias_t)


def _router_kernel(x_ref, g_ref, rw_ref, rb_ref, tril_ref, hn_ref, gate_ref, e_ref, rank_ref,
                   cnt_ref, run_ref):
    @pl.when(pl.program_id(0) == 0)
    def _():
        run_ref[...] = jnp.zeros(run_ref.shape, F32)

    x = x_ref[...]
    hn = x * lax.rsqrt(jnp.mean(x * x, axis=-1, keepdims=True) + RMS_EPS) * g_ref[...]
    hn_ref[...] = hn
    a1, a2, a3 = _split3(hn)
    b1, b2, b3 = _split3(rw_ref[...])
    logits = (_dot(a1, b1) + _dot(a1, b2) + _dot(a2, b1) + _dot(a2, b2) + _dot(a1, b3)
              + _dot(a3, b1)) + rb_ref[...]
    lane = lax.broadcasted_iota(I32, logits.shape, 1).astype(F32)
    onehot = jnp.zeros(logits.shape, F32)
    vals, idxs = [], []
    s = logits
    for _ in range(TOP_K):
        mx = jnp.max(s, axis=-1, keepdims=True)
        first = jnp.min(jnp.where(s == mx, lane, BIG_INDEX), axis=-1, keepdims=True)
        hit = lane == first
        onehot = jnp.where(hit, 1.0, onehot)
        s = jnp.where(hit, NEG, s)
        vals.append(mx)
        idxs.append(first)
    ex = [jnp.exp(v - vals[0]) for v in vals]
    den = ex[0] + ex[1] + ex[2] + ex[3]
    before = _dot(tril_ref[...], onehot.astype(BF16)) + run_ref[0:1, :]
    gate = jnp.zeros(logits.shape, F32)
    e_out = jnp.zeros(logits.shape, I32)
    rank = jnp.zeros(logits.shape, I32)
    for kk in range(TOP_K):
        slot = lane == kk
        rk = jnp.sum(jnp.where(lane == idxs[kk], before, 0.0), axis=-1, keepdims=True)
        gate = jnp.where(slot, ex[kk] / den, gate)
        e_out = jnp.where(slot, idxs[kk].astype(I32), e_out)
        rank = jnp.where(slot, rk.astype(I32), rank)
    gate_ref[...] = gate
    e_ref[...] = e_out
    rank_ref[...] = rank
    run_ref[...] = run_ref[...] + jnp.sum(onehot, axis=0, keepdims=True)
    cnt_ref[...] = run_ref[...]


def router(x, norm_g, router_w_pad, router_b_pad, tb):
    t, d = x.shape
    tril = (lax.broadcasted_iota(I32, (tb, tb), 0) > lax.broadcasted_iota(I32, (tb, tb), 1)).astype(BF16)
    row = lambda width: pl.BlockSpec((tb, width), lambda i: (i, 0))
    const = lambda shape: pl.BlockSpec(shape, lambda i: (0, 0))
    return pl.pallas_call(
        _router_kernel,
        out_shape=[jax.ShapeDtypeStruct((t, d), F32), jax.ShapeDtypeStruct((t, LANES), F32),
                   jax.ShapeDtypeStruct((t, LANES), I32), jax.ShapeDtypeStruct((t, LANES), I32),
                   jax.ShapeDtypeStruct((SUBLANES, LANES), F32)],
        grid=(t // tb,),
        in_specs=[row(d), const((1, d)), const((d, LANES)), const((1, LANES)), const((tb, tb))],
        out_specs=[row(d), row(LANES), row(LANES), row(LANES), const((SUBLANES, LANES))],
        scratch_shapes=[pltpu.VMEM((SUBLANES, LANES), F32)],
        compiler_params=pltpu.CompilerParams(
            dimension_semantics=("arbitrary",),
            vmem_limit_bytes=_vmem_limit(10 * _nbytes((tb, d), F32), 8 * _nbytes((d, LANES), F32))),
        name="router",
    )(x, norm_g.reshape(1, d), router_w_pad, router_b_pad, tril)


def _row_copy(src_hbm, buf, sem, src_row, dst_row):
    return pltpu.make_async_copy(src_hbm.at[pl.ds(src_row, 1), :], buf.at[pl.ds(dst_row, 1), :], sem)


def _gather_rows_kernel(idx_ref, nxt_ref, src_hbm, o_ref, buf, sem):
    i = pl.program_id(0)
    n = buf.shape[1]
    slot = i % 2

    def issue(ids_ref, s):
        def body(r, c):
            _row_copy(src_hbm, buf.at[s], sem.at[s], ids_ref[0, 0, r], r).start()
            return c
        lax.fori_loop(0, n, body, 0)

    @pl.when(i == 0)
    def _():
        issue(idx_ref, 0)

    @pl.when(i + 1 < pl.num_programs(0))
    def _():
        issue(nxt_ref, 1 - slot)

    def drain(r, c):
        _row_copy(src_hbm, buf.at[slot], sem.at[slot], 0, r).wait()
        return c

    lax.fori_loop(0, n, drain, 0)
    o_ref[...] = buf[slot].astype(o_ref.dtype)


def gather_rows(src, row_idx, tm, out_dtype):
    n_rows = row_idx.shape[0]
    d = src.shape[1]
    n_steps = n_rows // tm
    ids = row_idx.reshape(n_steps, 1, tm)
    return pl.pallas_call(
        _gather_rows_kernel,
        out_shape=jax.ShapeDtypeStruct((n_rows, d), out_dtype),
        grid=(n_steps,),
        in_specs=[pl.BlockSpec((1, 1, tm), lambda i: (i, 0, 0), memory_space=pltpu.SMEM),
                  pl.BlockSpec((1, 1, tm), lambda i: (jnp.minimum(i + 1, n_steps - 1), 0, 0),
                               memory_space=pltpu.SMEM),
                  pl.BlockSpec(memory_space=pl.ANY)],
        out_specs=pl.BlockSpec((tm, d), lambda i: (i, 0)),
        scratch_shapes=[pltpu.VMEM((2, tm, d), src.dtype), pltpu.SemaphoreType.DMA((2,))],
        compiler_params=pltpu.CompilerParams(
            dimension_semantics=("arbitrary",),
            vmem_limit_bytes=_vmem_limit(3 * _nbytes((tm, d), F32), 2 * _nbytes((tm, d), out_dtype))),
        name="gather_rows",
    )(ids, ids, src)


def _combine_kernel(idx_ref, y_hbm, x_ref, gate_ref, o_ref, buf, sem, *, tb):
    n = buf.shape[0]

    def issue(r, c):
        _row_copy(y_hbm, buf, sem, idx_ref[0, 0, r], r).start()
        return c

    def drain(r, c):
        _row_copy(y_hbm, buf, sem, 0, r).wait()
        return c

    lax.fori_loop(0, n, issue, 0)
    lax.fori_loop(0, n, drain, 0)
    gate = gate_ref[...]
    ffn = gate[:, 0:1] * buf[0:tb, :]
    for kk in range(1, TOP_K):
        ffn = ffn + gate[:, kk:kk + 1] * buf[kk * tb:(kk + 1) * tb, :]
    o_ref[...] = x_ref[...] + ffn


def moe_combine(yb, x, gate, dest, tb):
    t, d = x.shape
    idx = dest.reshape(t // tb, tb, TOP_K).transpose(0, 2, 1).reshape(t // tb, 1, TOP_K * tb)
    kern = functools.partial(_combine_kernel, tb=tb)
    return pl.pallas_call(
        kern,
        out_shape=jax.ShapeDtypeStruct((t, d), F32),
        grid=(t // tb,),
        in_specs=[pl.BlockSpec((1, 1, TOP_K * tb), lambda i: (i, 0, 0), memory_space=pltpu.SMEM),
                  pl.BlockSpec(memory_space=pl.ANY),
                  pl.BlockSpec((tb, d), lambda i: (i, 0)),
                  pl.BlockSpec((tb, LANES), lambda i: (i, 0))],
        out_specs=pl.BlockSpec((tb, d), lambda i: (i, 0)),
        scratch_shapes=[pltpu.VMEM((TOP_K * tb, d), F32), pltpu.SemaphoreType.DMA],
        compiler_params=pltpu.CompilerParams(
            dimension_semantics=("arbitrary",),
            vmem_limit_bytes=_vmem_limit((TOP_K + 6) * _nbytes((tb, d), F32))),
        name="moe_combine",
    )(idx, yb, x, gate)


def _first_of_expert(be_ref, i):
    return (i == 0) | (be_ref[i] != be_ref[jnp.maximum(i - 1, 0)])


def _moe_gu_kernel(be_ref, nu_ref, x_ref, wg_ref, wu_ref, bg_ref, bu_ref, o_ref, wgb_ref, wub_ref):
    i = pl.program_id(1)
    used = i < nu_ref[0]

    @pl.when(used & _first_of_expert(be_ref, i))
    def _():
        wgb_ref[...] = wg_ref[...].astype(BF16)
        wub_ref[...] = wu_ref[...].astype(BF16)

    @pl.when(used)
    def _():
        x = x_ref[...]
        g = jnp.minimum(_dot(x, wgb_ref[...]) + bg_ref[...], SWIGLU_LIMIT)
        u = jnp.clip(_dot(x, wub_ref[...]) + bu_ref[...], -SWIGLU_LIMIT, SWIGLU_LIMIT)
        o_ref[...] = ((u + 1.0) * (g * jax.nn.sigmoid(SWIGLU_ALPHA * g))).astype(o_ref.dtype)

    @pl.when(jnp.logical_not(used))
    def _():
        o_ref[...] = jnp.zeros(o_ref.shape, o_ref.dtype)


def moe_gate_up(xb, w_gu, b_gu, block_e, n_used, n_blocks):
    n_rows, d = xb.shape
    tm, tn = MOE_TM, MOE_TN_GU
    nj = D_FF // tn
    blk = lambda i, nu: jnp.minimum(i, nu[0] - 1)
    grid_spec = pltpu.PrefetchScalarGridSpec(
        num_scalar_prefetch=2,
        grid=(nj, n_blocks),
        in_specs=[
            pl.BlockSpec((tm, d), lambda j, i, be, nu: (blk(i, nu), 0)),
            pl.BlockSpec((None, d, tn), lambda j, i, be, nu: (be[blk(i, nu)], 0, j)),
            pl.BlockSpec((None, d, tn), lambda j, i, be, nu: (be[blk(i, nu)], 0, nj + j)),
            pl.BlockSpec((None, 1, tn), lambda j, i, be, nu: (be[blk(i, nu)], 0, j)),
            pl.BlockSpec((None, 1, tn), lambda j, i, be, nu: (be[blk(i, nu)], 0, nj + j)),
        ],
        out_specs=pl.BlockSpec((tm, tn), lambda j, i, be, nu: (i, j)),
        scratch_shapes=[pltpu.VMEM((d, tn), BF16), pltpu.VMEM((d, tn), BF16)],
    )
    return pl.pallas_call(
        _moe_gu_kernel,
        out_shape=jax.ShapeDtypeStruct((n_rows, D_FF), BF16),
        grid_spec=grid_spec,
        compiler_params=pltpu.CompilerParams(
            dimension_semantics=("arbitrary", "arbitrary"),
            vmem_limit_bytes=_vmem_limit(2 * _nbytes((tm, d), BF16), 4 * _nbytes((d, tn), F32),
                                         2 * _nbytes((d, tn), BF16), 8 * _nbytes((tm, tn), F32))),
        name="moe_gate_up",
    )(block_e, n_used, xb, w_gu, w_gu, b_gu.reshape(N_EXPERTS, 1, 2 * D_FF),
      b_gu.reshape(N_EXPERTS, 1, 2 * D_FF))


def _moe_down_kernel(be_ref, nu_ref, a_ref, w_ref, b_ref, o_ref, wb_ref):
    i = pl.program_id(1)
    used = i < nu_ref[0]

    @pl.when(used & _first_of_expert(be_ref, i))
    def _():
        wb_ref[...] = w_ref[...].astype(BF16)

    @pl.when(used)
    def _():
        o_ref[...] = _dot(a_ref[...], wb_ref[...]) + b_ref[...]

    @pl.when(jnp.logical_not(used))
    def _():
        o_ref[...] = jnp.zeros(o_ref.shape, o_ref.dtype)


def moe_down(act, w_down, b_down, block_e, n_used, n_blocks):
    n_rows, dff = act.shape
    tm, tn = MOE_TM, MOE_TN_DOWN
    blk = lambda i, nu: jnp.minimum(i, nu[0] - 1)
    grid_spec = pltpu.PrefetchScalarGridSpec(
        num_scalar_prefetch=2,
        grid=(D_MODEL // tn, n_blocks),
        in_specs=[
            pl.BlockSpec((tm, dff), lambda j, i, be, nu: (blk(i, nu), 0)),
            pl.BlockSpec((None, dff, tn), lambda j, i, be, nu: (be[blk(i, nu)], 0, j)),
            pl.BlockSpec((None, 1, tn), lambda j, i, be, nu: (be[blk(i, nu)], 0, j)),
        ],
        out_specs=pl.BlockSpec((tm, tn), lambda j, i, be, nu: (i, j)),
        scratch_shapes=[pltpu.VMEM((dff, tn), BF16)],
    )
    return pl.pallas_call(
        _moe_down_kernel,
        out_shape=jax.ShapeDtypeStruct((n_rows, D_MODEL), F32),
        grid_spec=grid_spec,
        compiler_params=pltpu.CompilerParams(
            dimension_semantics=("arbitrary", "arbitrary"),
            vmem_limit_bytes=_vmem_limit(2 * _nbytes((tm, dff), BF16), 2 * _nbytes((dff, tn), F32),
                                         _nbytes((dff, tn), BF16), 4 * _nbytes((tm, tn), F32))),
        name="moe_down",
    )(block_e, n_used, act, w_down, b_down.reshape(N_EXPERTS, 1, D_MODEL))


def _t5_bucket(dist):
    n = jnp.maximum(dist, 0)
    max_exact = N_BUCKETS // 2
    nf = jnp.maximum(n, 1).astype(F32)
    large = max_exact + (jnp.log(nf / max_exact) / math.log(MAX_DISTANCE / max_exact)
                         * (N_BUCKETS - max_exact)).astype(I32)
    return jnp.where(n < max_exact, n, jnp.minimum(large, N_BUCKETS - 1))


def _bias_lookup(rel_bias, dist):
    onehot = (_t5_bucket(dist)[..., None] == jnp.arange(N_BUCKETS, dtype=I32)).astype(F32)
    return jnp.einsum("...b,bh->...h", onehot, rel_bias.astype(F32), precision=lax.Precision.HIGHEST)


def _state_to_tiles(s):
    n = s.shape[0]
    s = s.reshape(n, RW_LANE_TILES, LANES // RW_HD, RW_HD, RW_HD).transpose(0, 1, 3, 2, 4)
    return s.reshape(n, RW_LANE_TILES, RW_HD, LANES)


def _tiles_to_state(s):
    n = s.shape[0]
    s = s.reshape(n, RW_LANE_TILES, RW_HD, LANES // RW_HD, RW_HD).transpose(0, 1, 3, 2, 4)
    return s.reshape(n, RW_HEADS, RW_HD, RW_HD)


def _pad_rows(w, row0, n_rows):
    return jnp.zeros((n_rows, w.shape[1]), BF16).at[row0:row0 + w.shape[0]].set(w.astype(BF16))


def kernel(x_prompt, x_sample, cache_k, cache_v, page_table, state_rwkv, state_shift, norm1_g, w_in,
           q_norm_g, k_norm_g, rel_bias, mu_shift, w0, w_lora_up, a0, a_lora_up, g_lora_up, k_k, k_a,
           r_k, lnx_w, lnx_b, w_branch_a, w_branch_b, w_out, norm2_g, router_w, router_b, w_gu, b_gu,
           w_down, b_down):
    n_p, t_p, d = x_prompt.shape
    n_s, t_s, _ = x_sample.shape
    tok_p, tok_s = n_p * t_p, n_s * t_s
    n_tok = tok_p + tok_s
    x = jnp.concatenate([x_prompt.reshape(tok_p, d), x_sample.reshape(tok_s, d)], axis=0)

    h = rmsnorm_rows(x, norm1_g, BF16, 256)
    cb = ATT_W // MM_TN
    p_q = matmul(h, w_in, ATT_W, col_block0=0, name="proj_q")
    p_k = matmul(h, w_in, ATT_W, col_block0=cb, name="proj_k")
    p_v = matmul(h, w_in, ATT_W, col_block0=2 * cb, name="proj_v")
    p_rw = matmul(h, w_in, RW_PAD, col_block0=3 * cb, name="proj_rw")
    w_gate = w_in[:, 3 * ATT_W + RW_COLS:]
    p_ga = matmul(h, w_gate, D_MODEL, col_block0=0, name="proj_gate_a")
    p_gb = matmul(h, w_gate, D_MODEL, col_block0=D_MODEL // MM_TN, name="proj_gate_b")

    head_rows = n_tok * ATT_HEADS
    q_n = rmsnorm_rows(p_q.reshape(head_rows, ATT_HD), q_norm_g, BF16, 2048).reshape(n_tok, ATT_W)
    k_n = rmsnorm_rows(p_k.reshape(head_rows, ATT_HD), k_norm_g, F32, 2048).reshape(n_tok, ATT_W)

    past = page_table.shape[1] * cache_k.shape[1]
    n_bias = 6
    ri = jnp.arange(MOBA_BLOCK, dtype=I32)
    dist = (jnp.arange(n_bias, dtype=I32)[:, None, None] * MOBA_BLOCK + ri[None, :, None] - ri[None, None, :])
    bias_tiles = _bias_lookup(rel_bias, dist).transpose(3, 0, 1, 2)
    bias_tiles = jnp.where(dist[None] >= 0, bias_tiles, NEG)
    o_att_p = moba_prompt(q_n, k_n, p_v, bias_tiles, n_p, t_p)

    q_s = q_n[tok_p:].reshape(n_s, t_s, ATT_HEADS, ATT_HD)
    q_all = q_s.transpose(0, 3, 2, 1).reshape(n_s, ATT_HD, ATT_HEADS * t_s)
    new_rows = lambda a: a[tok_p:].reshape(n_s, t_s * ATT_HEADS, ATT_HD)
    key_pos = jnp.arange(past + LANES, dtype=I32)
    q_idx = jnp.arange(t_s, dtype=I32)
    d_s = past + q_idx[None, :] - key_pos[:, None]
    bias_t = _bias_lookup(rel_bias, d_s).transpose(0, 2, 1).reshape(past + LANES, ATT_HEADS * t_s)
    pool = cache_k.shape[0]
    o_att_s = moba_sample(q_all, cache_k.reshape(pool, -1, ATT_HD), cache_v.reshape(pool, -1, ATT_HD),
                          new_rows(k_n), new_rows(p_v), page_table, bias_t)
    o_att = jnp.concatenate([o_att_p, o_att_s.reshape(tok_s, ATT_W).astype(BF16)], axis=0)

    shift_s = jnp.pad(state_shift.astype(F32), ((0, 0), (0, RW_PAD - RW_COLS)))
    row = lambda a: a.reshape(1, -1).astype(F32)
    vecs = (jnp.pad(row(mu_shift), ((0, 0), (0, RW_PAD - RW_COLS))), row(w0), row(a0), row(k_k), row(k_a),
            row(r_k))
    loras = (_pad_rows(w_lora_up, 0, LORA_PAD), _pad_rows(a_lora_up, W_LORA, LORA_PAD),
             _pad_rows(g_lora_up, W_LORA + A_LORA, LORA_PAD))
    lane_i = jnp.arange(LANES, dtype=I32)
    bd = (lane_i[:, None] // RW_HD == lane_i[None, :] // RW_HD).astype(BF16)
    ipat = (jnp.arange(RW_HD, dtype=I32)[:, None] == lane_i[None, :] % RW_HD).astype(F32)
    def rwkv_group(row0, n, t, shift, s0, n_group, n_tok):
        prep = rwkv_prep(p_rw, row0, n, t, shift, vecs, loras, bd)
        tiles = lambda a: a.reshape(n, t, RW_LANE_TILES, LANES)
        y, s = rwkv_scan([tiles(a) for a in prep[:6]], s0, bd, ipat, n_group, n_tok)
        o = rwkv_post(y.reshape(n * t, RW_W), prep[6], prep[7], row(lnx_w), row(lnx_b), bd, 256)
        return o, s

    o_rw_p, s_p = rwkv_group(0, n_p, t_p, jnp.zeros((n_p, RW_PAD), F32),
                             jnp.zeros((n_p, RW_LANE_TILES, RW_HD, LANES), F32), n_p, 64)
    o_rw_s, s_s = rwkv_group(tok_p, n_s, t_s, shift_s, _state_to_tiles(state_rwkv.astype(F32)), 4, t_s)
    o_rw = jnp.concatenate([o_rw_p, o_rw_s], axis=0)

    m_a = matmul(o_att, w_branch_a, D_MODEL, epilogue="gate", gate=p_ga, name="branch_a")
    merged = matmul(o_rw, w_branch_b, D_MODEL, epilogue="gate", gate=p_gb, addend=m_a,
                    out_dtype=BF16, name="branch_b")
    x1 = matmul(merged, w_out, D_MODEL, epilogue="residual", addend=x, name="out_proj")

    rw_pad = jnp.pad(router_w.astype(F32), ((0, 0), (0, LANES - N_EXPERTS)))
    rb_pad = jnp.pad(router_b.astype(F32).reshape(1, -1), ((0, 0), (0, LANES - N_EXPERTS)),
                     constant_values=NEG)
    hn, gate, e_sel, rank, counts = router(x1, norm2_g, rw_pad, rb_pad, 256)
    tm = MOE_TM
    n_assign = n_tok * TOP_K
    n_blocks = -(-(n_assign + N_EXPERTS * (tm - 1)) // tm)
    counts = counts[0, :N_EXPERTS].astype(I32)
    padded = (counts + tm - 1) // tm * tm
    pad_end = jnp.cumsum(padded)
    pad_start = pad_end - padded
    dest = pad_start[e_sel[:, :TOP_K]] + rank[:, :TOP_K]
    row_tok = jnp.zeros((n_blocks * tm,), I32).at[dest.reshape(-1)].set(
        jnp.arange(n_assign, dtype=I32) // TOP_K)
    block_e = jnp.minimum(jnp.searchsorted(pad_end, jnp.arange(n_blocks, dtype=I32) * tm, side="right"),
                          N_EXPERTS - 1).astype(I32)
    n_used = (pad_end[-1:] // tm).astype(I32)
    xb = gather_rows(hn, row_tok, tm, BF16)
    act = moe_gate_up(xb, w_gu, b_gu, block_e, n_used, n_blocks)
    yb = moe_down(act, w_down, b_down, block_e, n_used, n_blocks)
    y_out = moe_combine(yb, x1, gate, dest, 128)

    lead = lambda a, lo, n, t: a[lo:lo + n * t].reshape(n, t, ATT_HEADS, ATT_HD)
    last_rw = lambda lo, n, t: p_rw[lo + t - 1:lo + n * t:t, :RW_COLS]
    return (y_out[:tok_p].reshape(n_p, t_p, d), y_out[tok_p:].reshape(n_s, t_s, d),
            lead(k_n, 0, n_p, t_p), lead(p_v, 0, n_p, t_p),
            lead(k_n, tok_p, n_s, t_s), lead(p_v, tok_p, n_s, t_s),
            _tiles_to_state(s_p), last_rw(0, n_p, t_p),
            _tiles_to_state(s_s), last_rw(tok_p, n_s, t_s))
```

```python
import functools
import math

import jax
import jax.numpy as jnp
from jax import lax
from jax.experimental import pallas as pl
from jax.experimental.pallas import tpu as pltpu

F32 = jnp.float32
BF16 = jnp.bfloat16
I32 = jnp.int32

LANES = 128
SUBLANES = 8
VMEM_CAP_BYTES = 60 * 1024 * 1024
VMEM_SLACK_BYTES = 6 * 1024 * 1024

D_MODEL = 4096
ATT_HD = 128
ATT_W = D_MODEL // 2
ATT_HEADS = ATT_W // ATT_HD
MOBA_BLOCK = 256
MOBA_TOPK = 3
ATT_SCALE = ATT_HD ** -0.5
N_BUCKETS = 32
MAX_DISTANCE = 1024
RW_HD = 64
RW_W = D_MODEL // 2
RW_HEADS = RW_W // RW_HD
W_LORA = max(32, int(round(RW_W ** 0.5 * 1.8 / 32)) * 32)
A_LORA = max(32, int(round(RW_W ** 0.5 * 1.8 / 32)) * 32)
G_LORA = max(32, int(round(RW_W ** 0.8 * 0.6 / 32)) * 32)
RW_COLS = 3 * RW_W + W_LORA + A_LORA + G_LORA
LORA_COLS = W_LORA + A_LORA + G_LORA
LNX_EPS = 64e-5
N_EXPERTS = 32
TOP_K = 4
D_FF = D_MODEL
SWIGLU_LIMIT = 7.0
SWIGLU_ALPHA = 1.702
RMS_EPS = 1e-6
NEG = -1e30
BIG_INDEX = 1e9

RW_LANE_TILES = RW_W // LANES
MM_TM = 512
MM_TN = 512
RW_PAD = -(-RW_COLS // MM_TN) * MM_TN
LORA_PAD = RW_PAD - 3 * RW_W
MOE_TM = 256
MOE_TN_GU = 512
MOE_TN_DOWN = 1024
SCAN_CHUNK = 4
MOBA_GROUP = 4
SAMPLE_PAGES_PER_STEP = 8
PREP_TB = 64


def _vmem_limit(*buffer_bytes):
    return int(min(VMEM_CAP_BYTES, sum(buffer_bytes) + VMEM_SLACK_BYTES))


def _nbytes(shape, dtype):
    return math.prod(shape) * jnp.dtype(dtype).itemsize


def _split3(x):
    x1 = x.astype(BF16)
    r1 = x - x1.astype(F32)
    x2 = r1.astype(BF16)
    x3 = (r1 - x2.astype(F32)).astype(BF16)
    return x1, x2, x3


def _dot(a, b):
    return jnp.dot(a, b, preferred_element_type=F32)


def _rmsnorm_kernel(x_ref, g_ref, o_ref):
    x = x_ref[...].astype(F32)
    y = x * lax.rsqrt(jnp.mean(x * x, axis=-1, keepdims=True) + RMS_EPS)
    o_ref[...] = (y * g_ref[...]).astype(o_ref.dtype)


def rmsnorm_rows(x, g, out_dtype, tb):
    m, d = x.shape
    return pl.pallas_call(
        _rmsnorm_kernel,
        out_shape=jax.ShapeDtypeStruct((m, d), out_dtype),
        grid=(m // tb,),
        in_specs=[pl.BlockSpec((tb, d), lambda i: (i, 0)),
                  pl.BlockSpec((1, d), lambda i: (0, 0))],
        out_specs=pl.BlockSpec((tb, d), lambda i: (i, 0)),
        compiler_params=pltpu.CompilerParams(
            dimension_semantics=("arbitrary",),
            vmem_limit_bytes=_vmem_limit(4 * _nbytes((tb, d), F32))),
        name="rmsnorm_rows",
    )(x, g.reshape(1, d).astype(F32))


def _matmul_kernel(*refs, epilogue, has_addend):
    a_ref, w_ref = refs[0], refs[1]
    o_ref, wb_ref = refs[-2], refs[-1]

    @pl.when(pl.program_id(1) == 0)
    def _():
        wb_ref[...] = w_ref[...].astype(BF16)

    acc = _dot(a_ref[...], wb_ref[...])
    if epilogue == "gate":
        acc = jax.nn.sigmoid(refs[2][...]) * acc
        if has_addend:
            acc = acc + refs[3][...]
    elif epilogue == "residual":
        acc = refs[2][...] + acc
    o_ref[...] = acc.astype(o_ref.dtype)


def matmul(a, w, n_out, *, col_block0=0, epilogue=None, gate=None, addend=None,
           out_dtype=F32, name="matmul"):
    m, k = a.shape
    tm, tn = MM_TM, MM_TN
    extra, extra_specs = [], []
    io_spec = pl.BlockSpec((tm, tn), lambda j, i: (i, j))
    if epilogue == "gate":
        extra.append(gate)
        extra_specs.append(io_spec)
        if addend is not None:
            extra.append(addend)
            extra_specs.append(io_spec)
    elif epilogue == "residual":
        extra.append(addend)
        extra_specs.append(io_spec)
    kern = functools.partial(_matmul_kernel, epilogue=epilogue,
                             has_addend=(epilogue == "gate" and addend is not None))
    return pl.pallas_call(
        kern,
        out_shape=jax.ShapeDtypeStruct((m, n_out), out_dtype),
        grid=(n_out // tn, m // tm),
        in_specs=[pl.BlockSpec((tm, k), lambda j, i: (i, 0)),
                  pl.BlockSpec((k, tn), lambda j, i: (0, j + col_block0))] + extra_specs,
        out_specs=io_spec,
        scratch_shapes=[pltpu.VMEM((k, tn), BF16)],
        compiler_params=pltpu.CompilerParams(
            dimension_semantics=("arbitrary", "arbitrary"),
            vmem_limit_bytes=_vmem_limit(
                2 * _nbytes((tm, k), BF16), 2 * _nbytes((k, tn), F32), _nbytes((k, tn), BF16),
                2 * (2 + len(extra)) * _nbytes((tm, tn), F32))),
        name=name,
    )(a, w, *extra)


def _segsum(x, bd):
    outs = []
    for c in range(x.shape[-1] // LANES):
        x1, x2, x3 = _split3(x[:, c * LANES:(c + 1) * LANES])
        outs.append(_dot(x1, bd) + _dot(x2, bd) + _dot(x3, bd))
    return jnp.concatenate(outs, axis=-1)


def _softplus(z):
    return jnp.maximum(z, 0.0) + jnp.log(1.0 + jnp.exp(-jnp.abs(z)))


def _rwkv_prep_kernel(rw_ref, shift_ref, mu_ref, w0_ref, a0_ref, kk_g_ref, ka_ref, rk_ref,
                      wl_ref, al_ref, gl_ref, bd_ref,
                      w_o, r_o, a_o, b_o, kx_o, v_o, g_o, bonus_o, carry_ref, *, seq_len):
    rw = rw_ref[...]
    tb = rw.shape[0]
    row = lax.broadcasted_iota(I32, rw.shape, 0)
    prev = pltpu.roll(rw, 1, axis=0)
    if seq_len >= tb:
        @pl.when(pl.program_id(0) == 0)
        def _():
            carry_ref[...] = jnp.zeros(carry_ref.shape, F32)

        starts = pl.program_id(0) % (seq_len // tb) == 0
        first = jnp.where(starts, shift_ref[...], carry_ref[...])
        prev = jnp.where(row == 0, first, prev)
        carry_ref[...] = rw[tb - 1:tb, :]
    else:
        for s in range(tb // seq_len):
            prev = jnp.where(row == s * seq_len, shift_ref[s:s + 1, :], prev)
    m = rw + (prev - rw) * mu_ref[...]
    r = m[:, 0:RW_W]
    k = m[:, RW_W:2 * RW_W]
    v = m[:, 2 * RW_W:3 * RW_W]
    x = m[:, 3 * RW_W:]
    bd = bd_ref[...]
    u = w0_ref[...] + _dot(jnp.tanh(x).astype(BF16), wl_ref[...])
    w_log = -_softplus(-u) - 0.5
    decay = jnp.exp(-jnp.exp(w_log))
    a = jax.nn.sigmoid(a0_ref[...] + _dot(x.astype(BF16), al_ref[...]))
    g = _dot(jax.nn.sigmoid(x).astype(BF16), gl_ref[...])
    kk = k * kk_g_ref[...]
    kk = kk / jnp.maximum(jnp.sqrt(_segsum(kk * kk, bd)), 1e-12)
    kx = k * (1.0 + (a - 1.0) * ka_ref[...])
    bvec = kk * a
    w_o[...] = decay
    r_o[...] = r
    a_o[...] = -kk
    b_o[...] = bvec
    kx_o[...] = kx
    v_o[...] = v
    g_o[...] = g
    bonus_o[...] = _segsum(r * kx * rk_ref[...], bd) * v


def rwkv_prep(rw, row0, n_seq, seq_len, shift, vecs, loras, bd):
    tb = PREP_TB
    t = n_seq * seq_len
    blk0 = row0 // tb
    const = lambda shape: pl.BlockSpec(shape, lambda i: (0, 0))
    if seq_len >= tb:
        per_seq = seq_len // tb
        shift = shift.reshape(n_seq, 1, RW_PAD)
        shift_spec = pl.BlockSpec((None, 1, RW_PAD), lambda i: (i // per_seq, 0, 0))
    else:
        shift_spec = pl.BlockSpec((tb // seq_len, RW_PAD), lambda i: (i, 0))
    out_spec = pl.BlockSpec((tb, RW_W), lambda i: (i, 0))
    mu, w0, a0, k_k, k_a, r_k = vecs
    n_out = 8
    return pl.pallas_call(
        functools.partial(_rwkv_prep_kernel, seq_len=seq_len),
        out_shape=[jax.ShapeDtypeStruct((t, RW_W), F32)] * n_out,
        grid=(t // tb,),
        in_specs=[pl.BlockSpec((tb, RW_PAD), lambda i: (i + blk0, 0)), shift_spec, const((1, RW_PAD))]
                 + [const((1, RW_W))] * 5 + [const((LORA_PAD, RW_W))] * 3 + [const((LANES, LANES))],
        out_specs=[out_spec] * n_out,
        scratch_shapes=[pltpu.VMEM((1, RW_PAD), F32)],
        compiler_params=pltpu.CompilerParams(
            dimension_semantics=("arbitrary",),
            vmem_limit_bytes=_vmem_limit(6 * _nbytes((tb, RW_PAD), F32),
                                         2 * n_out * _nbytes((tb, RW_W), F32),
                                         6 * _nbytes((LORA_PAD, RW_W), BF16),
                                         16 * _nbytes((tb, RW_W), F32))),
        name="rwkv_prep",
    )(rw, shift, mu, w0, a0, k_k, k_a, r_k, *loras, bd)


def _rowmul_bf16(xb, row):
    pack = 2 * SUBLANES
    rb = jnp.broadcast_to(row, (pack, LANES)).astype(BF16)
    return (xb.reshape(xb.shape[0] // pack, pack, LANES) * rb[None]).reshape(xb.shape)


def _rwkv_scan_kernel(w_ref, r_ref, a_ref, b_ref, kx_ref, v_ref, s0_ref, bd_ref, ipat_ref,
                      y_ref, s_ref, lhs_ref, res_ref, *, n_group, n_tok):
    @pl.when(pl.program_id(1) == 0)
    def _():
        s_ref[...] = s0_ref[...]

    ipat = ipat_ref[...]
    ipat_b = ipat.astype(BF16)
    seg = 3 * RW_HD
    blocks = [(g, c) for g in range(n_group) for c in range(RW_LANE_TILES)]

    def put_y(g, t_out, c, spread):
        y_ref[g, t_out, pl.ds(c, 1), :] = jnp.sum(spread * ipat, axis=0, keepdims=True)

    def step(t, carry):
        tp = jnp.maximum(t - 1, 0)
        for g in range(n_group):
            a, rp, vv = a_ref[g, t], r_ref[g, tp], v_ref[g, t]
            w, b, kx = w_ref[g, t], b_ref[g, t], kx_ref[g, t]
            for c0 in range(0, RW_LANE_TILES, SCAN_CHUNK):
                lo = (g * RW_LANE_TILES + c0) * seg
                hi = lo + SCAN_CHUNK * seg
                for c in range(c0, c0 + SCAN_CHUNK):
                    sb = s_ref[g, c].astype(BF16)
                    base = (g * RW_LANE_TILES + c) * seg
                    lhs_ref[base:base + RW_HD, :] = _rowmul_bf16(sb, a[c:c + 1])
                    lhs_ref[base + RW_HD:base + 2 * RW_HD, :] = _rowmul_bf16(sb, rp[c:c + 1])
                    lhs_ref[base + 2 * RW_HD:base + seg, :] = _rowmul_bf16(ipat_b, vv[c:c + 1])
                res_ref[lo:hi, :] = _dot(lhs_ref[lo:hi, :], bd_ref[...])
                for c in range(c0, c0 + SCAN_CHUNK):
                    base = (g * RW_LANE_TILES + c) * seg
                    sa = res_ref[base:base + RW_HD, :]
                    vx = res_ref[base + 2 * RW_HD:base + seg, :]
                    put_y(g, tp, c, res_ref[base + RW_HD:base + 2 * RW_HD, :])
                    s_ref[g, c] = s_ref[g, c] * w[c:c + 1] + sa * b[c:c + 1] + vx * kx[c:c + 1]
        return carry

    lax.fori_loop(0, n_tok, step, 0)

    n_rows = len(blocks) * RW_HD
    for g, c in blocks:
        base = (g * RW_LANE_TILES + c) * RW_HD
        lhs_ref[base:base + RW_HD, :] = _rowmul_bf16(s_ref[g, c].astype(BF16), r_ref[g, n_tok - 1][c:c + 1])
    res_ref[0:n_rows, :] = _dot(lhs_ref[0:n_rows, :], bd_ref[...])
    for g, c in blocks:
        base = (g * RW_LANE_TILES + c) * RW_HD
        put_y(g, n_tok - 1, c, res_ref[base:base + RW_HD, :])


def rwkv_scan(seq_inputs, s0, bd, ipat, n_group, n_tok):
    n, t = seq_inputs[0].shape[:2]
    tok = pl.BlockSpec((n_group, n_tok, RW_LANE_TILES, LANES), lambda i, j: (i, j, 0, 0))
    st = pl.BlockSpec((n_group, RW_LANE_TILES, RW_HD, LANES), lambda i, j: (i, 0, 0, 0))
    lhs_rows = n_group * RW_LANE_TILES * 3 * RW_HD
    kern = functools.partial(_rwkv_scan_kernel, n_group=n_group, n_tok=n_tok)
    tok_bytes = _nbytes((n_group, n_tok, RW_LANE_TILES, LANES), F32)
    st_bytes = _nbytes((n_group, RW_LANE_TILES, RW_HD, LANES), F32)
    return pl.pallas_call(
        kern,
        out_shape=[jax.ShapeDtypeStruct((n, t, RW_LANE_TILES, LANES), F32),
                   jax.ShapeDtypeStruct(s0.shape, F32)],
        grid=(n // n_group, t // n_tok),
        in_specs=[tok] * 6 + [st, pl.BlockSpec((LANES, LANES), lambda i, j: (0, 0)),
                              pl.BlockSpec((RW_HD, LANES), lambda i, j: (0, 0))],
        out_specs=[tok, st],
        scratch_shapes=[pltpu.VMEM((lhs_rows, LANES), BF16), pltpu.VMEM((lhs_rows, LANES), F32)],
        compiler_params=pltpu.CompilerParams(
            dimension_semantics=("arbitrary", "arbitrary"),
            vmem_limit_bytes=_vmem_limit(14 * tok_bytes, 4 * st_bytes,
                                         _nbytes((lhs_rows, LANES), BF16),
                                         _nbytes((lhs_rows, LANES), F32))),
        name="rwkv_scan",
    )(*seq_inputs, s0, bd, ipat)


def _rwkv_post_kernel(y_ref, g_ref, bonus_ref, lw_ref, lb_ref, bd_ref, o_ref):
    y = y_ref[...]
    bd = bd_ref[...]
    mu = _segsum(y, bd) * (1.0 / RW_HD)
    d = y - mu
    var = _segsum(d * d, bd) * (1.0 / RW_HD)
    yn = d * lax.rsqrt(var + LNX_EPS) * lw_ref[...] + lb_ref[...]
    o_ref[...] = ((yn + bonus_ref[...]) * g_ref[...]).astype(o_ref.dtype)


def rwkv_post(y, g, bonus, lnx_w, lnx_b, bd, tb):
    t = y.shape[0]
    row = pl.BlockSpec((tb, RW_W), lambda i: (i, 0))
    const = lambda shape: pl.BlockSpec(shape, lambda i: (0, 0))
    return pl.pallas_call(
        _rwkv_post_kernel,
        out_shape=jax.ShapeDtypeStruct((t, RW_W), BF16),
        grid=(t // tb,),
        in_specs=[row, row, row, const((1, RW_W)), const((1, RW_W)), const((LANES, LANES))],
        out_specs=row,
        compiler_params=pltpu.CompilerParams(
            dimension_semantics=("arbitrary",),
            vmem_limit_bytes=_vmem_limit(16 * _nbytes((tb, RW_W), F32))),
        name="rwkv_post",
    )(y, g, bonus, lnx_w, lnx_b, bd)


def _top_mask(score, index, n_pick, axis):
    index = index.astype(F32)
    sel = jnp.zeros(score.shape, F32)
    for _ in range(n_pick):
        mx = jnp.max(score, axis=axis, keepdims=True)
        first = jnp.min(jnp.where(score == mx, index, BIG_INDEX), axis=axis, keepdims=True)
        hit = index == first
        sel = jnp.maximum(sel, jnp.where(hit, jnp.where(mx > 0.5 * NEG, 1.0, 0.0), 0.0))
        score = jnp.where(hit, NEG, score)
    return sel


def _moba_prompt_kernel(q_ref, k_ref, v_ref, bias_ref, expand_ref, o_ref, kmean_ref, m_ref, l_ref,
                        acc_ref, *, n_blocks, n_bias):
    i = pl.program_id(2)
    bs = MOBA_BLOCK
    nt = (((1,), (1,)), ((), ()))

    @pl.when(i == 0)
    def _():
        kmean_ref[...] = jnp.zeros(kmean_ref.shape, F32)
        kmean_ref[0:n_blocks, :] = jnp.mean(k_ref[...].reshape(n_blocks, bs, ATT_HD), axis=1)

    q = q_ref[...]
    km1, km2, km3 = _split3(kmean_ref[...])
    score = (lax.dot_general(q, km1, nt, preferred_element_type=F32)
             + lax.dot_general(q, km2, nt, preferred_element_type=F32)
             + lax.dot_general(q, km3, nt, preferred_element_type=F32))
    lane = lax.broadcasted_iota(I32, score.shape, 1)
    sel = _top_mask(jnp.where(lane < i, score, NEG), lane, MOBA_TOPK, axis=1)
    attend = jnp.where(lane == i, 1.0, sel).astype(BF16)

    m_ref[...] = jnp.full(m_ref.shape, NEG, F32)
    l_ref[...] = jnp.zeros(l_ref.shape, F32)
    acc_ref[...] = jnp.zeros(acc_ref.shape, F32)
    gw = MOBA_GROUP * bs

    for g in range(n_blocks // MOBA_GROUP):
        @pl.when(g * MOBA_GROUP <= i)
        def _(g=g):
            rows = slice(g * gw, (g + 1) * gw)
            kb = k_ref[rows, :].astype(BF16)
            vb = v_ref[rows, :].astype(BF16)
            tiles = [jnp.clip(i - (g * MOBA_GROUP + u), 0, n_bias - 1) for u in range(MOBA_GROUP)]
            half = bs // 2
            for qs in (slice(0, half), slice(half, bs)):
                s = lax.dot_general(q[qs], kb, nt, preferred_element_type=F32) * ATT_SCALE
                bias = jnp.concatenate([bias_ref[tile, qs, :] for tile in tiles], axis=-1)
                on = _dot(attend[qs], expand_ref[g])
                s = jnp.where(on > 0.5, s + bias, NEG)
                m_old = m_ref[qs, :]
                m_new = jnp.maximum(m_old, jnp.max(s, axis=-1, keepdims=True))
                alpha = jnp.exp(m_old - m_new)
                p = jnp.exp(s - m_new)
                l_ref[qs, :] = alpha * l_ref[qs, :] + jnp.sum(p, axis=-1, keepdims=True)
                acc_ref[qs, :] = alpha * acc_ref[qs, :] + _dot(p.astype(BF16), vb)
                m_ref[qs, :] = m_new

    o_ref[...] = (acc_ref[...] / l_ref[...]).astype(o_ref.dtype)


def moba_prompt(q, k, v, bias_tiles, n, t):
    bs = MOBA_BLOCK
    n_blocks = t // bs
    n_groups = n_blocks // MOBA_GROUP
    n_bias = bias_tiles.shape[1]
    gw = MOBA_GROUP * bs
    blk_of_key = jnp.arange(n_groups * gw, dtype=I32).reshape(n_groups, 1, gw) // bs
    expand = (jnp.arange(LANES, dtype=I32)[None, :, None] == blk_of_key).astype(BF16)
    qo = pl.BlockSpec((bs, ATT_HD), lambda s, h, i: (s * n_blocks + i, h))
    kv = pl.BlockSpec((t, ATT_HD), lambda s, h, i: (s, h))
    kern = functools.partial(_moba_prompt_kernel, n_blocks=n_blocks, n_bias=n_bias)
    return pl.pallas_call(
        kern,
        out_shape=jax.ShapeDtypeStruct((n * t, ATT_W), BF16),
        grid=(n, ATT_HEADS, n_blocks),
        in_specs=[qo, kv, kv,
                  pl.BlockSpec((None, n_bias, bs, bs), lambda s, h, i: (h, 0, 0, 0)),
                  pl.BlockSpec((n_groups, LANES, gw), lambda s, h, i: (0, 0, 0))],
        out_specs=qo,
        scratch_shapes=[pltpu.VMEM((LANES, ATT_HD), F32), pltpu.VMEM((bs, 1), F32),
                        pltpu.VMEM((bs, 1), F32), pltpu.VMEM((bs, ATT_HD), F32)],
        compiler_params=pltpu.CompilerParams(
            dimension_semantics=("arbitrary", "arbitrary", "arbitrary"),
            vmem_limit_bytes=_vmem_limit(4 * _nbytes((t, ATT_HD), F32),
                                         2 * _nbytes((n_bias, bs, bs), F32),
                                         2 * _nbytes((n_groups, LANES, gw), BF16),
                                         12 * _nbytes((bs, gw), F32))),
        name="moba_prompt",
    )(q, k, v, bias_tiles, expand)


def _moba_sample_kernel(pt_ref, qall_ref, *refs, n_pages, pps, n_new):
    del pt_ref
    kp_refs, vp_refs = refs[:pps], refs[pps:2 * pps]
    kn_ref, vn_ref, bias_ref, o_ref, r_ref, lg_ref, p_ref, acc_ref = refs[2 * pps:]
    j = pl.program_id(1)
    nh = ATT_HEADS
    page = kp_refs[0].shape[0] // nh
    past = n_pages * page
    n_ksteps = n_pages // pps
    n_blocks = past // MOBA_BLOCK
    rows_all = lg_ref.shape[0]
    lane = lax.broadcasted_iota(I32, (1, LANES), 1)
    head_of_lane = lane >> (n_new.bit_length() - 1)
    head_cols = [jnp.where(head_of_lane == h, 1.0, 0.0) for h in range(nh)]

    def raw_logits(src_ref, n_tok):
        r_ref[0:n_tok * nh, :] = _dot(src_ref[...].astype(BF16), qall_ref[...])
        out = r_ref[pl.ds(0, n_tok, stride=nh), :] * head_cols[0]
        for h in range(1, nh):
            out = out + r_ref[pl.ds(h, n_tok, stride=nh), :] * head_cols[h]
        return out

    @pl.when(j < n_ksteps)
    def _():
        for u in range(pps):
            row0 = pl.multiple_of((j * pps + u) * page, page)
            lg_ref[pl.ds(row0, page), :] = raw_logits(kp_refs[u], page)

    @pl.when(j == n_ksteps - 1)
    def _():
        new = raw_logits(kn_ref, n_new) * ATT_SCALE + bias_ref[past:past + n_new, :]
        jn = lax.broadcasted_iota(I32, new.shape, 0)
        qi = lax.broadcasted_iota(I32, new.shape, 1) & (n_new - 1)
        score = jnp.mean(lg_ref[0:past, :].reshape(n_blocks, MOBA_BLOCK, LANES), axis=1)
        sub = lax.broadcasted_iota(I32, score.shape, 0)
        sel = _top_mask(score, sub, MOBA_TOPK, axis=0)
        for kb in range(n_blocks):
            rows = slice(kb * MOBA_BLOCK, (kb + 1) * MOBA_BLOCK)
            s = lg_ref[rows, :] * ATT_SCALE + bias_ref[rows, :]
            lg_ref[rows, :] = jnp.where(sel[kb:kb + 1, :] > 0.5, s, NEG)
        lg_ref[past:past + n_new, :] = jnp.where(jn <= qi, new, NEG)
        lg_ref[past + n_new:rows_all, :] = jnp.full((rows_all - past - n_new, LANES), NEG, F32)
        lg = lg_ref[...]
        p = jnp.exp(lg - jnp.max(lg, axis=0, keepdims=True))
        p_ref[...] = p / jnp.sum(p, axis=0, keepdims=True)

    @pl.when(j == n_ksteps)
    def _():
        acc_ref[...] = jnp.zeros(acc_ref.shape, F32)

    def add_pv(p_rows, v_of_head):
        pt = p_rows.T.astype(BF16)
        for h in range(nh):
            lo = (h // 2) * 2 * n_new
            o = _dot(pt[lo:lo + 2 * n_new, :], v_of_head(h))
            off = (h % 2) * n_new
            acc_ref[h * n_new:(h + 1) * n_new, :] += o[off:off + n_new, :]

    @pl.when(j >= n_ksteps)
    def _():
        for u in range(pps):
            row0 = pl.multiple_of(((j - n_ksteps) * pps + u) * page, page)
            add_pv(p_ref[pl.ds(row0, page), :],
                   lambda h, u=u: vp_refs[u][pl.ds(h, page, stride=nh), :].astype(BF16))

    @pl.when(j == 2 * n_ksteps - 1)
    def _():
        def v_new(h):
            rows = vn_ref[pl.ds(h, n_new, stride=nh), :]
            return jnp.concatenate([rows, jnp.zeros((rows_all - past - n_new, ATT_HD), F32)],
                                   axis=0).astype(BF16)

        add_pv(p_ref[past:rows_all, :], v_new)
        acc = acc_ref[...]
        for h in range(nh):
            o_ref[:, h * ATT_HD:(h + 1) * ATT_HD] = acc[h * n_new:(h + 1) * n_new, :]


def moba_sample(q_all, cache_k, cache_v, k_new, v_new, page_table, bias_t):
    n_seq, n_pages = page_table.shape
    pps = SAMPLE_PAGES_PER_STEP
    page_rows = cache_k.shape[1]
    n_new = LANES // ATT_HEADS
    past = n_pages * page_rows // ATT_HEADS
    rows_all = past + LANES
    n_ksteps = n_pages // pps
    kern = functools.partial(_moba_sample_kernel, n_pages=n_pages, pps=pps, n_new=n_new)

    def page_spec(u, for_v):
        def index(s, j, pt):
            step = jnp.maximum(j - n_ksteps, 0) if for_v else jnp.minimum(j, n_ksteps - 1)
            return (pt[s * n_pages + step * pps + u], 0, 0)
        return pl.BlockSpec((None, page_rows, ATT_HD), index)

    per_seq = lambda rows, cols: pl.BlockSpec((None, rows, cols), lambda s, j, pt: (s, 0, 0))
    grid_spec = pltpu.PrefetchScalarGridSpec(
        num_scalar_prefetch=1,
        grid=(n_seq, 2 * n_ksteps),
        in_specs=[per_seq(ATT_HD, LANES)]
                 + [page_spec(u, False) for u in range(pps)] + [page_spec(u, True) for u in range(pps)]
                 + [per_seq(n_new * ATT_HEADS, ATT_HD), per_seq(n_new * ATT_HEADS, ATT_HD),
                    pl.BlockSpec((rows_all, LANES), lambda s, j, pt: (0, 0))],
        out_specs=per_seq(n_new, ATT_W),
        scratch_shapes=[pltpu.VMEM((page_rows, LANES), F32), pltpu.VMEM((rows_all, LANES), F32),
                        pltpu.VMEM((rows_all, LANES), F32), pltpu.VMEM((LANES, ATT_HD), F32)],
    )
    return pl.pallas_call(
        kern,
        out_shape=jax.ShapeDtypeStruct((n_seq, n_new, ATT_W), F32),
        grid_spec=grid_spec,
        compiler_params=pltpu.CompilerParams(
            dimension_semantics=("arbitrary", "arbitrary"),
            vmem_limit_bytes=_vmem_limit((4 * pps + 4) * _nbytes((page_rows, ATT_HD), F32),
                                         8 * _nbytes((rows_all, LANES), F32))),
        name="moba_sample",
    )(page_table.reshape(-1), q_all, *([cache_k] * pps), *([cache_v] * pps), k_new, v_new, bias_t)


def _router_kernel(x_ref, g_ref, rw_ref, rb_ref, tril_ref, hn_ref, gate_ref, e_ref, rank_ref,
                   cnt_ref, run_ref):
    @pl.when(pl.program_id(0) == 0)
    def _():
        run_ref[...] = jnp.zeros(run_ref.shape, F32)

    x = x_ref[...]
    hn = x * lax.rsqrt(jnp.mean(x * x, axis=-1, keepdims=True) + RMS_EPS) * g_ref[...]
    hn_ref[...] = hn
    a1, a2, a3 = _split3(hn)
    b1, b2, b3 = _split3(rw_ref[...])
    logits = (_dot(a1, b1) + _dot(a1, b2) + _dot(a2, b1) + _dot(a2, b2) + _dot(a1, b3)
              + _dot(a3, b1)) + rb_ref[...]
    lane = lax.broadcasted_iota(I32, logits.shape, 1).astype(F32)
    onehot = jnp.zeros(logits.shape, F32)
    vals, idxs = [], []
    s = logits
    for _ in range(TOP_K):
        mx = jnp.max(s, axis=-1, keepdims=True)
        first = jnp.min(jnp.where(s == mx, lane, BIG_INDEX), axis=-1, keepdims=True)
        hit = lane == first
        onehot = jnp.where(hit, 1.0, onehot)
        s = jnp.where(hit, NEG, s)
        vals.append(mx)
        idxs.append(first)
    ex = [jnp.exp(v - vals[0]) for v in vals]
    den = ex[0] + ex[1] + ex[2] + ex[3]
    before = _dot(tril_ref[...], onehot.astype(BF16)) + run_ref[0:1, :]
    gate = jnp.zeros(logits.shape, F32)
    e_out = jnp.zeros(logits.shape, I32)
    rank = jnp.zeros(logits.shape, I32)
    for kk in range(TOP_K):
        slot = lane == kk
        rk = jnp.sum(jnp.where(lane == idxs[kk], before, 0.0), axis=-1, keepdims=True)
        gate = jnp.where(slot, ex[kk] / den, gate)
        e_out = jnp.where(slot, idxs[kk].astype(I32), e_out)
        rank = jnp.where(slot, rk.astype(I32), rank)
    gate_ref[...] = gate
    e_ref[...] = e_out
    rank_ref[...] = rank
    run_ref[...] = run_ref[...] + jnp.sum(onehot, axis=0, keepdims=True)
    cnt_ref[...] = run_ref[...]


def router(x, norm_g, router_w_pad, router_b_pad, tb):
    t, d = x.shape
    tril = (lax.broadcasted_iota(I32, (tb, tb), 0) > lax.broadcasted_iota(I32, (tb, tb), 1)).astype(BF16)
    row = lambda width: pl.BlockSpec((tb, width), lambda i: (i, 0))
    const = lambda shape: pl.BlockSpec(shape, lambda i: (0, 0))
    return pl.pallas_call(
        _router_kernel,
        out_shape=[jax.ShapeDtypeStruct((t, d), F32), jax.ShapeDtypeStruct((t, LANES), F32),
                   jax.ShapeDtypeStruct((t, LANES), I32), jax.ShapeDtypeStruct((t, LANES), I32),
                   jax.ShapeDtypeStruct((SUBLANES, LANES), F32)],
        grid=(t // tb,),
        in_specs=[row(d), const((1, d)), const((d, LANES)), const((1, LANES)), const((tb, tb))],
        out_specs=[row(d), row(LANES), row(LANES), row(LANES), const((SUBLANES, LANES))],
        scratch_shapes=[pltpu.VMEM((SUBLANES, LANES), F32)],
        compiler_params=pltpu.CompilerParams(
            dimension_semantics=("arbitrary",),
            vmem_limit_bytes=_vmem_limit(10 * _nbytes((tb, d), F32), 8 * _nbytes((d, LANES), F32))),
        name="router",
    )(x, norm_g.reshape(1, d), router_w_pad, router_b_pad, tril)


def _row_copy(src_hbm, buf, sem, src_row, dst_row):
    return pltpu.make_async_copy(src_hbm.at[pl.ds(src_row, 1), :], buf.at[pl.ds(dst_row, 1), :], sem)


def _gather_rows_kernel(idx_ref, nxt_ref, src_hbm, o_ref, buf, sem):
    i = pl.program_id(0)
    n = buf.shape[1]
    slot = i % 2

    def issue(ids_ref, s):
        def body(r, c):
            _row_copy(src_hbm, buf.at[s], sem.at[s], ids_ref[0, 0, r], r).start()
            return c
        lax.fori_loop(0, n, body, 0)

    @pl.when(i == 0)
    def _():
        issue(idx_ref, 0)

    @pl.when(i + 1 < pl.num_programs(0))
    def _():
        issue(nxt_ref, 1 - slot)

    def drain(r, c):
        _row_copy(src_hbm, buf.at[slot], sem.at[slot], 0, r).wait()
        return c

    lax.fori_loop(0, n, drain, 0)
    o_ref[...] = buf[slot].astype(o_ref.dtype)


def gather_rows(src, row_idx, tm, out_dtype):
    n_rows = row_idx.shape[0]
    d = src.shape[1]
    n_steps = n_rows // tm
    ids = row_idx.reshape(n_steps, 1, tm)
    return pl.pallas_call(
        _gather_rows_kernel,
        out_shape=jax.ShapeDtypeStruct((n_rows, d), out_dtype),
        grid=(n_steps,),
        in_specs=[pl.BlockSpec((1, 1, tm), lambda i: (i, 0, 0), memory_space=pltpu.SMEM),
                  pl.BlockSpec((1, 1, tm), lambda i: (jnp.minimum(i + 1, n_steps - 1), 0, 0),
                               memory_space=pltpu.SMEM),
                  pl.BlockSpec(memory_space=pl.ANY)],
        out_specs=pl.BlockSpec((tm, d), lambda i: (i, 0)),
        scratch_shapes=[pltpu.VMEM((2, tm, d), src.dtype), pltpu.SemaphoreType.DMA((2,))],
        compiler_params=pltpu.CompilerParams(
            dimension_semantics=("arbitrary",),
            vmem_limit_bytes=_vmem_limit(3 * _nbytes((tm, d), F32), 2 * _nbytes((tm, d), out_dtype))),
        name="gather_rows",
    )(ids, ids, src)


def _combine_kernel(idx_ref, y_hbm, x_ref, gate_ref, o_ref, buf, sem, *, tb):
    n = buf.shape[0]

    def issue(r, c):
        _row_copy(y_hbm, buf, sem, idx_ref[0, 0, r], r).start()
        return c

    def drain(r, c):
        _row_copy(y_hbm, buf, sem, 0, r).wait()
        return c

    lax.fori_loop(0, n, issue, 0)
    lax.fori_loop(0, n, drain, 0)
    gate = gate_ref[...]
    ffn = gate[:, 0:1] * buf[0:tb, :]
    for kk in range(1, TOP_K):
        ffn = ffn + gate[:, kk:kk + 1] * buf[kk * tb:(kk + 1) * tb, :]
    o_ref[...] = x_ref[...] + ffn


def moe_combine(yb, x, gate, dest, tb):
    t, d = x.shape
    idx = dest.reshape(t // tb, tb, TOP_K).transpose(0, 2, 1).reshape(t // tb, 1, TOP_K * tb)
    kern = functools.partial(_combine_kernel, tb=tb)
    return pl.pallas_call(
        kern,
        out_shape=jax.ShapeDtypeStruct((t, d), F32),
        grid=(t // tb,),
        in_specs=[pl.BlockSpec((1, 1, TOP_K * tb), lambda i: (i, 0, 0), memory_space=pltpu.SMEM),
                  pl.BlockSpec(memory_space=pl.ANY),
                  pl.BlockSpec((tb, d), lambda i: (i, 0)),
                  pl.BlockSpec((tb, LANES), lambda i: (i, 0))],
        out_specs=pl.BlockSpec((tb, d), lambda i: (i, 0)),
        scratch_shapes=[pltpu.VMEM((TOP_K * tb, d), F32), pltpu.SemaphoreType.DMA],
        compiler_params=pltpu.CompilerParams(
            dimension_semantics=("arbitrary",),
            vmem_limit_bytes=_vmem_limit((TOP_K + 6) * _nbytes((tb, d), F32))),
        name="moe_combine",
    )(idx, yb, x, gate)


def _first_of_expert(be_ref, i):
    return (i == 0) | (be_ref[i] != be_ref[jnp.maximum(i - 1, 0)])


def _moe_gu_kernel(be_ref, nu_ref, x_ref, wg_ref, wu_ref, bg_ref, bu_ref, o_ref, wgb_ref, wub_ref):
    i = pl.program_id(1)
    used = i < nu_ref[0]

    @pl.when(used & _first_of_expert(be_ref, i))
    def _():
        wgb_ref[...] = wg_ref[...].astype(BF16)
        wub_ref[...] = wu_ref[...].astype(BF16)

    @pl.when(used)
    def _():
        x = x_ref[...]
        g = jnp.minimum(_dot(x, wgb_ref[...]) + bg_ref[...], SWIGLU_LIMIT)
        u = jnp.clip(_dot(x, wub_ref[...]) + bu_ref[...], -SWIGLU_LIMIT, SWIGLU_LIMIT)
        o_ref[...] = ((u + 1.0) * (g * jax.nn.sigmoid(SWIGLU_ALPHA * g))).astype(o_ref.dtype)

    @pl.when(jnp.logical_not(used))
    def _():
        o_ref[...] = jnp.zeros(o_ref.shape, o_ref.dtype)


def moe_gate_up(xb, w_gu, b_gu, block_e, n_used, n_blocks):
    n_rows, d = xb.shape
    tm, tn = MOE_TM, MOE_TN_GU
    nj = D_FF // tn
    blk = lambda i, nu: jnp.minimum(i, nu[0] - 1)
    grid_spec = pltpu.PrefetchScalarGridSpec(
        num_scalar_prefetch=2,
        grid=(nj, n_blocks),
        in_specs=[
            pl.BlockSpec((tm, d), lambda j, i, be, nu: (blk(i, nu), 0)),
            pl.BlockSpec((None, d, tn), lambda j, i, be, nu: (be[blk(i, nu)], 0, j)),
            pl.BlockSpec((None, d, tn), lambda j, i, be, nu: (be[blk(i, nu)], 0, nj + j)),
            pl.BlockSpec((None, 1, tn), lambda j, i, be, nu: (be[blk(i, nu)], 0, j)),
            pl.BlockSpec((None, 1, tn), lambda j, i, be, nu: (be[blk(i, nu)], 0, nj + j)),
        ],
        out_specs=pl.BlockSpec((tm, tn), lambda j, i, be, nu: (i, j)),
        scratch_shapes=[pltpu.VMEM((d, tn), BF16), pltpu.VMEM((d, tn), BF16)],
    )
    return pl.pallas_call(
        _moe_gu_kernel,
        out_shape=jax.ShapeDtypeStruct((n_rows, D_FF), BF16),
        grid_spec=grid_spec,
        compiler_params=pltpu.CompilerParams(
            dimension_semantics=("arbitrary", "arbitrary"),
            vmem_limit_bytes=_vmem_limit(2 * _nbytes((tm, d), BF16), 4 * _nbytes((d, tn), F32),
                                         2 * _nbytes((d, tn), BF16), 8 * _nbytes((tm, tn), F32))),
        name="moe_gate_up",
    )(block_e, n_used, xb, w_gu, w_gu, b_gu.reshape(N_EXPERTS, 1, 2 * D_FF),
      b_gu.reshape(N_EXPERTS, 1, 2 * D_FF))


def _moe_down_kernel(be_ref, nu_ref, a_ref, w_ref, b_ref, o_ref, wb_ref):
    i = pl.program_id(1)
    used = i < nu_ref[0]

    @pl.when(used & _first_of_expert(be_ref, i))
    def _():
        wb_ref[...] = w_ref[...].astype(BF16)

    @pl.when(used)
    def _():
        o_ref[...] = _dot(a_ref[...], wb_ref[...]) + b_ref[...]

    @pl.when(jnp.logical_not(used))
    def _():
        o_ref[...] = jnp.zeros(o_ref.shape, o_ref.dtype)


def moe_down(act, w_down, b_down, block_e, n_used, n_blocks):
    n_rows, dff = act.shape
    tm, tn = MOE_TM, MOE_TN_DOWN
    blk = lambda i, nu: jnp.minimum(i, nu[0] - 1)
    grid_spec = pltpu.PrefetchScalarGridSpec(
        num_scalar_prefetch=2,
        grid=(D_MODEL // tn, n_blocks),
        in_specs=[
            pl.BlockSpec((tm, dff), lambda j, i, be, nu: (blk(i, nu), 0)),
            pl.BlockSpec((None, dff, tn), lambda j, i, be, nu: (be[blk(i, nu)], 0, j)),
            pl.BlockSpec((None, 1, tn), lambda j, i, be, nu: (be[blk(i, nu)], 0, j)),
        ],
        out_specs=pl.BlockSpec((tm, tn), lambda j, i, be, nu: (i, j)),
        scratch_shapes=[pltpu.VMEM((dff, tn), BF16)],
    )
    return pl.pallas_call(
        _moe_down_kernel,
        out_shape=jax.ShapeDtypeStruct((n_rows, D_MODEL), F32),
        grid_spec=grid_spec,
        compiler_params=pltpu.CompilerParams(
            dimension_semantics=("arbitrary", "arbitrary"),
            vmem_limit_bytes=_vmem_limit(2 * _nbytes((tm, dff), BF16), 2 * _nbytes((dff, tn), F32),
                                         _nbytes((dff, tn), BF16), 4 * _nbytes((tm, tn), F32))),
        name="moe_down",
    )(block_e, n_used, act, w_down, b_down.reshape(N_EXPERTS, 1, D_MODEL))


def _t5_bucket(dist):
    n = jnp.maximum(dist, 0)
    max_exact = N_BUCKETS // 2
    nf = jnp.maximum(n, 1).astype(F32)
    large = max_exact + (jnp.log(nf / max_exact) / math.log(MAX_DISTANCE / max_exact)
                         * (N_BUCKETS - max_exact)).astype(I32)
    return jnp.where(n < max_exact, n, jnp.minimum(large, N_BUCKETS - 1))


def _bias_lookup(rel_bias, dist):
    onehot = (_t5_bucket(dist)[..., None] == jnp.arange(N_BUCKETS, dtype=I32)).astype(F32)
    return jnp.einsum("...b,bh->...h", onehot, rel_bias.astype(F32), precision=lax.Precision.HIGHEST)


def _state_to_tiles(s):
    n = s.shape[0]
    s = s.reshape(n, RW_LANE_TILES, LANES // RW_HD, RW_HD, RW_HD).transpose(0, 1, 3, 2, 4)
    return s.reshape(n, RW_LANE_TILES, RW_HD, LANES)


def _tiles_to_state(s):
    n = s.shape[0]
    s = s.reshape(n, RW_LANE_TILES, RW_HD, LANES // RW_HD, RW_HD).transpose(0, 1, 3, 2, 4)
    return s.reshape(n, RW_HEADS, RW_HD, RW_HD)


def _pad_rows(w, row0, n_rows):
    return jnp.zeros((n_rows, w.shape[1]), BF16).at[row0:row0 + w.shape[0]].set(w.astype(BF16))


def kernel(x_prompt, x_sample, cache_k, cache_v, page_table, state_rwkv, state_shift, norm1_g, w_in,
           q_norm_g, k_norm_g, rel_bias, mu_shift, w0, w_lora_up, a0, a_lora_up, g_lora_up, k_k, k_a,
           r_k, lnx_w, lnx_b, w_branch_a, w_branch_b, w_out, norm2_g, router_w, router_b, w_gu, b_gu,
           w_down, b_down):
    n_p, t_p, d = x_prompt.shape
    n_s, t_s, _ = x_sample.shape
    tok_p, tok_s = n_p * t_p, n_s * t_s
    n_tok = tok_p + tok_s
    x = jnp.concatenate([x_prompt.reshape(tok_p, d), x_sample.reshape(tok_s, d)], axis=0)

    h = rmsnorm_rows(x, norm1_g, BF16, 256)
    cb = ATT_W // MM_TN
    p_q = matmul(h, w_in, ATT_W, col_block0=0, name="proj_q")
    p_k = matmul(h, w_in, ATT_W, col_block0=cb, name="proj_k")
    p_v = matmul(h, w_in, ATT_W, col_block0=2 * cb, name="proj_v")
    p_rw = matmul(h, w_in, RW_PAD, col_block0=3 * cb, name="proj_rw")
    w_gate = w_in[:, 3 * ATT_W + RW_COLS:]
    p_ga = matmul(h, w_gate, D_MODEL, col_block0=0, name="proj_gate_a")
    p_gb = matmul(h, w_gate, D_MODEL, col_block0=D_MODEL // MM_TN, name="proj_gate_b")

    head_rows = n_tok * ATT_HEADS
    q_n = rmsnorm_rows(p_q.reshape(head_rows, ATT_HD), q_norm_g, BF16, 2048).reshape(n_tok, ATT_W)
    k_n = rmsnorm_rows(p_k.reshape(head_rows, ATT_HD), k_norm_g, F32, 2048).reshape(n_tok, ATT_W)

    past = page_table.shape[1] * cache_k.shape[1]
    n_bias = 6
    ri = jnp.arange(MOBA_BLOCK, dtype=I32)
    dist = (jnp.arange(n_bias, dtype=I32)[:, None, None] * MOBA_BLOCK + ri[None, :, None] - ri[None, None, :])
    bias_tiles = _bias_lookup(rel_bias, dist).transpose(3, 0, 1, 2)
    bias_tiles = jnp.where(dist[None] >= 0, bias_tiles, NEG)
    o_att_p = moba_prompt(q_n, k_n, p_v, bias_tiles, n_p, t_p)

    q_s = q_n[tok_p:].reshape(n_s, t_s, ATT_HEADS, ATT_HD)
    q_all = q_s.transpose(0, 3, 2, 1).reshape(n_s, ATT_HD, ATT_HEADS * t_s)
    new_rows = lambda a: a[tok_p:].reshape(n_s, t_s * ATT_HEADS, ATT_HD)
    key_pos = jnp.arange(past + LANES, dtype=I32)
    q_idx = jnp.arange(t_s, dtype=I32)
    d_s = past + q_idx[None, :] - key_pos[:, None]
    bias_t = _bias_lookup(rel_bias, d_s).transpose(0, 2, 1).reshape(past + LANES, ATT_HEADS * t_s)
    pool = cache_k.shape[0]
    o_att_s = moba_sample(q_all, cache_k.reshape(pool, -1, ATT_HD), cache_v.reshape(pool, -1, ATT_HD),
                          new_rows(k_n), new_rows(p_v), page_table, bias_t)
    o_att = jnp.concatenate([o_att_p, o_att_s.reshape(tok_s, ATT_W).astype(BF16)], axis=0)

    shift_s = jnp.pad(state_shift.astype(F32), ((0, 0), (0, RW_PAD - RW_COLS)))
    row = lambda a: a.reshape(1, -1).astype(F32)
    vecs = (jnp.pad(row(mu_shift), ((0, 0), (0, RW_PAD - RW_COLS))), row(w0), row(a0), row(k_k), row(k_a),
            row(r_k))
    loras = (_pad_rows(w_lora_up, 0, LORA_PAD), _pad_rows(a_lora_up, W_LORA, LORA_PAD),
             _pad_rows(g_lora_up, W_LORA + A_LORA, LORA_PAD))
    lane_i = jnp.arange(LANES, dtype=I32)
    bd = (lane_i[:, None] // RW_HD == lane_i[None, :] // RW_HD).astype(BF16)
    ipat = (jnp.arange(RW_HD, dtype=I32)[:, None] == lane_i[None, :] % RW_HD).astype(F32)
    def rwkv_group(row0, n, t, shift, s0, n_group, n_tok):
        prep = rwkv_prep(p_rw, row0, n, t, shift, vecs, loras, bd)
        tiles = lambda a: a.reshape(n, t, RW_LANE_TILES, LANES)
        y, s = rwkv_scan([tiles(a) for a in prep[:6]], s0, bd, ipat, n_group, n_tok)
        o = rwkv_post(y.reshape(n * t, RW_W), prep[6], prep[7], row(lnx_w), row(lnx_b), bd, 256)
        return o, s

    o_rw_p, s_p = rwkv_group(0, n_p, t_p, jnp.zeros((n_p, RW_PAD), F32),
                             jnp.zeros((n_p, RW_LANE_TILES, RW_HD, LANES), F32), n_p, 64)
    o_rw_s, s_s = rwkv_group(tok_p, n_s, t_s, shift_s, _state_to_tiles(state_rwkv.astype(F32)), 4, t_s)
    o_rw = jnp.concatenate([o_rw_p, o_rw_s], axis=0)

    m_a = matmul(o_att, w_branch_a, D_MODEL, epilogue="gate", gate=p_ga, name="branch_a")
    merged = matmul(o_rw, w_branch_b, D_MODEL, epilogue="gate", gate=p_gb, addend=m_a,
                    out_dtype=BF16, name="branch_b")
    x1 = matmul(merged, w_out, D_MODEL, epilogue="residual", addend=x, name="out_proj")

    rw_pad = jnp.pad(router_w.astype(F32), ((0, 0), (0, LANES - N_EXPERTS)))
    rb_pad = jnp.pad(router_b.astype(F32).reshape(1, -1), ((0, 0), (0, LANES - N_EXPERTS)),
                     constant_values=NEG)
    hn, gate, e_sel, rank, counts = router(x1, norm2_g, rw_pad, rb_pad, 256)
    tm = MOE_TM
    n_assign = n_tok * TOP_K
    n_blocks = -(-(n_assign + N_EXPERTS * (tm - 1)) // tm)
    counts = counts[0, :N_EXPERTS].astype(I32)
    padded = (counts + tm - 1) // tm * tm
    pad_end = jnp.cumsum(padded)
    pad_start = pad_end - padded
    dest = pad_start[e_sel[:, :TOP_K]] + rank[:, :TOP_K]
    row_tok = jnp.zeros((n_blocks * tm,), I32).at[dest.reshape(-1)].set(
        jnp.arange(n_assign, dtype=I32) // TOP_K)
    block_e = jnp.minimum(jnp.searchsorted(pad_end, jnp.arange(n_blocks, dtype=I32) * tm, side="right"),
                          N_EXPERTS - 1).astype(I32)
    n_used = (pad_end[-1:] // tm).astype(I32)
    xb = gather_rows(hn, row_tok, tm, BF16)
    act = moe_gate_up(xb, w_gu, b_gu, block_e, n_used, n_blocks)
    yb = moe_down(act, w_down, b_down, block_e, n_used, n_blocks)
    y_out = moe_combine(yb, x1, gate, dest, 128)

    lead = lambda a, lo, n, t: a[lo:lo + n * t].reshape(n, t, ATT_HEADS, ATT_HD)
    last_rw = lambda lo, n, t: p_rw[lo + t - 1:lo + n * t:t, :RW_COLS]
    return (y_out[:tok_p].reshape(n_p, t_p, d), y_out[tok_p:].reshape(n_s, t_s, d),
            lead(k_n, 0, n_p, t_p), lead(p_v, 0, n_p, t_p),
            lead(k_n, tok_p, n_s, t_s), lead(p_v, tok_p, n_s, t_s),
            _tiles_to_state(s_p), last_rw(0, n_p, t_p),
            _tiles_to_state(s_s), last_rw(tok_p, n_s, t_s))
```

```python
import functools
import math

import jax
import jax.numpy as jnp
from jax import lax
from jax.experimental import pallas as pl
from jax.experimental.pallas import tpu as pltpu

F32 = jnp.float32
BF16 = jnp.bfloat16
I32 = jnp.int32

LANES = 128
SUBLANES = 8
VMEM_CAP_BYTES = 60 * 1024 * 1024
VMEM_SLACK_BYTES = 6 * 1024 * 1024

D_MODEL = 4096
ATT_HD = 128
ATT_W = D_MODEL // 2
ATT_HEADS = ATT_W // ATT_HD
MOBA_BLOCK = 256
MOBA_TOPK = 3
ATT_SCALE = ATT_HD ** -0.5
N_BUCKETS = 32
MAX_DISTANCE = 1024
RW_HD = 64
RW_W = D_MODEL // 2
RW_HEADS = RW_W // RW_HD
W_LORA = max(32, int(round(RW_W ** 0.5 * 1.8 / 32)) * 32)
A_LORA = max(32, int(round(RW_W ** 0.5 * 1.8 / 32)) * 32)
G_LORA = max(32, int(round(RW_W ** 0.8 * 0.6 / 32)) * 32)
RW_COLS = 3 * RW_W + W_LORA + A_LORA + G_LORA
LORA_COLS = W_LORA + A_LORA + G_LORA
LNX_EPS = 64e-5
N_EXPERTS = 32
TOP_K = 4
D_FF = D_MODEL
SWIGLU_LIMIT = 7.0
SWIGLU_ALPHA = 1.702
RMS_EPS = 1e-6
NEG = -1e30
BIG_INDEX = 1e9

RW_LANE_TILES = RW_W // LANES
MM_TM = 512
MM_TN = 512
RW_PAD = -(-RW_COLS // MM_TN) * MM_TN
LORA_PAD = RW_PAD - 3 * RW_W
MOE_TM = 256
MOE_TN_GU = 512
MOE_TN_DOWN = 1024
SCAN_CHUNK = 4
MOBA_GROUP = 4
SAMPLE_PAGES_PER_STEP = 8
PREP_TB = 64


def _vmem_limit(*buffer_bytes):
    return int(min(VMEM_CAP_BYTES, sum(buffer_bytes) + VMEM_SLACK_BYTES))


def _nbytes(shape, dtype):
    return math.prod(shape) * jnp.dtype(dtype).itemsize


def _split3(x):
    x1 = x.astype(BF16)
    r1 = x - x1.astype(F32)
    x2 = r1.astype(BF16)
    x3 = (r1 - x2.astype(F32)).astype(BF16)
    return x1, x2, x3


def _dot(a, b):
    return jnp.dot(a, b, preferred_element_type=F32)


def _rmsnorm_kernel(x_ref, g_ref, o_ref):
    x = x_ref[...].astype(F32)
    y = x * lax.rsqrt(jnp.mean(x * x, axis=-1, keepdims=True) + RMS_EPS)
    o_ref[...] = (y * g_ref[...]).astype(o_ref.dtype)


def rmsnorm_rows(x, g, out_dtype, tb):
    m, d = x.shape
    return pl.pallas_call(
        _rmsnorm_kernel,
        out_shape=jax.ShapeDtypeStruct((m, d), out_dtype),
        grid=(m // tb,),
        in_specs=[pl.BlockSpec((tb, d), lambda i: (i, 0)),
                  pl.BlockSpec((1, d), lambda i: (0, 0))],
        out_specs=pl.BlockSpec((tb, d), lambda i: (i, 0)),
        compiler_params=pltpu.CompilerParams(
            dimension_semantics=("arbitrary",),
            vmem_limit_bytes=_vmem_limit(4 * _nbytes((tb, d), F32))),
        name="rmsnorm_rows",
    )(x, g.reshape(1, d).astype(F32))


def _matmul_kernel(*refs, epilogue, has_addend):
    a_ref, w_ref = refs[0], refs[1]
    o_ref, wb_ref = refs[-2], refs[-1]

    @pl.when(pl.program_id(1) == 0)
    def _():
        wb_ref[...] = w_ref[...].astype(BF16)

    acc = _dot(a_ref[...], wb_ref[...])
    if epilogue == "gate":
        acc = jax.nn.sigmoid(refs[2][...]) * acc
        if has_addend:
            acc = acc + refs[3][...]
    elif epilogue == "residual":
        acc = refs[2][...] + acc
    o_ref[...] = acc.astype(o_ref.dtype)


def matmul(a, w, n_out, *, col_block0=0, epilogue=None, gate=None, addend=None,
           out_dtype=F32, name="matmul"):
    m, k = a.shape
    tm, tn = MM_TM, MM_TN
    extra, extra_specs = [], []
    io_spec = pl.BlockSpec((tm, tn), lambda j, i: (i, j))
    if epilogue == "gate":
        extra.append(gate)
        extra_specs.append(io_spec)
        if addend is not None:
            extra.append(addend)
            extra_specs.append(io_spec)
    elif epilogue == "residual":
        extra.append(addend)
        extra_specs.append(io_spec)
    kern = functools.partial(_matmul_kernel, epilogue=epilogue,
                             has_addend=(epilogue == "gate" and addend is not None))
    return pl.pallas_call(
        kern,
        out_shape=jax.ShapeDtypeStruct((m, n_out), out_dtype),
        grid=(n_out // tn, m // tm),
        in_specs=[pl.BlockSpec((tm, k), lambda j, i: (i, 0)),
                  pl.BlockSpec((k, tn), lambda j, i: (0, j + col_block0))] + extra_specs,
        out_specs=io_spec,
        scratch_shapes=[pltpu.VMEM((k, tn), BF16)],
        compiler_params=pltpu.CompilerParams(
            dimension_semantics=("arbitrary", "arbitrary"),
            vmem_limit_bytes=_vmem_limit(
                2 * _nbytes((tm, k), BF16), 2 * _nbytes((k, tn), F32), _nbytes((k, tn), BF16),
                2 * (2 + len(extra)) * _nbytes((tm, tn), F32))),
        name=name,
    )(a, w, *extra)


def _segsum(x, bd):
    outs = []
    for c in range(x.shape[-1] // LANES):
        x1, x2, x3 = _split3(x[:, c * LANES:(c + 1) * LANES])
        outs.append(_dot(x1, bd) + _dot(x2, bd) + _dot(x3, bd))
    return jnp.concatenate(outs, axis=-1)


def _softplus(z):
    return jnp.maximum(z, 0.0) + jnp.log(1.0 + jnp.exp(-jnp.abs(z)))


def _rwkv_prep_kernel(rw_ref, shift_ref, mu_ref, w0_ref, a0_ref, kk_g_ref, ka_ref, rk_ref,
                      wl_ref, al_ref, gl_ref, bd_ref,
                      w_o, r_o, a_o, b_o, kx_o, v_o, g_o, bonus_o, carry_ref, *, seq_len):
    rw = rw_ref[...]
    tb = rw.shape[0]
    row = lax.broadcasted_iota(I32, rw.shape, 0)
    prev = pltpu.roll(rw, 1, axis=0)
    if seq_len >= tb:
        @pl.when(pl.program_id(0) == 0)
        def _():
            carry_ref[...] = jnp.zeros(carry_ref.shape, F32)

        starts = pl.program_id(0) % (seq_len // tb) == 0
        first = jnp.where(starts, shift_ref[...], carry_ref[...])
        prev = jnp.where(row == 0, first, prev)
        carry_ref[...] = rw[tb - 1:tb, :]
    else:
        for s in range(tb // seq_len):
            prev = jnp.where(row == s * seq_len, shift_ref[s:s + 1, :], prev)
    m = rw + (prev - rw) * mu_ref[...]
    r = m[:, 0:RW_W]
    k = m[:, RW_W:2 * RW_W]
    v = m[:, 2 * RW_W:3 * RW_W]
    x = m[:, 3 * RW_W:]
    bd = bd_ref[...]
    u = w0_ref[...] + _dot(jnp.tanh(x).astype(BF16), wl_ref[...])
    w_log = -_softplus(-u) - 0.5
    decay = jnp.exp(-jnp.exp(w_log))
    a = jax.nn.sigmoid(a0_ref[...] + _dot(x.astype(BF16), al_ref[...]))
    g = _dot(jax.nn.sigmoid(x).astype(BF16), gl_ref[...])
    kk = k * kk_g_ref[...]
    kk = kk / jnp.maximum(jnp.sqrt(_segsum(kk * kk, bd)), 1e-12)
    kx = k * (1.0 + (a - 1.0) * ka_ref[...])
    bvec = kk * a
    w_o[...] = decay
    r_o[...] = r
    a_o[...] = -kk
    b_o[...] = bvec
    kx_o[...] = kx
    v_o[...] = v
    g_o[...] = g
    bonus_o[...] = _segsum(r * kx * rk_ref[...], bd) * v


def rwkv_prep(rw, row0, n_seq, seq_len, shift, vecs, loras, bd):
    tb = PREP_TB
    t = n_seq * seq_len
    blk0 = row0 // tb
    const = lambda shape: pl.BlockSpec(shape, lambda i: (0, 0))
    if seq_len >= tb:
        per_seq = seq_len // tb
        shift = shift.reshape(n_seq, 1, RW_PAD)
        shift_spec = pl.BlockSpec((None, 1, RW_PAD), lambda i: (i // per_seq, 0, 0))
    else:
        shift_spec = pl.BlockSpec((tb // seq_len, RW_PAD), lambda i: (i, 0))
    out_spec = pl.BlockSpec((tb, RW_W), lambda i: (i, 0))
    mu, w0, a0, k_k, k_a, r_k = vecs
    n_out = 8
    return pl.pallas_call(
        functools.partial(_rwkv_prep_kernel, seq_len=seq_len),
        out_shape=[jax.ShapeDtypeStruct((t, RW_W), F32)] * n_out,
        grid=(t // tb,),
        in_specs=[pl.BlockSpec((tb, RW_PAD), lambda i: (i + blk0, 0)), shift_spec, const((1, RW_PAD))]
                 + [const((1, RW_W))] * 5 + [const((LORA_PAD, RW_W))] * 3 + [const((LANES, LANES))],
        out_specs=[out_spec] * n_out,
        scratch_shapes=[pltpu.VMEM((1, RW_PAD), F32)],
        compiler_params=pltpu.CompilerParams(
            dimension_semantics=("arbitrary",),
            vmem_limit_bytes=_vmem_limit(6 * _nbytes((tb, RW_PAD), F32),
                                         2 * n_out * _nbytes((tb, RW_W), F32),
                                         6 * _nbytes((LORA_PAD, RW_W), BF16),
                                         16 * _nbytes((tb, RW_W), F32))),
        name="rwkv_prep",
    )(rw, shift, mu, w0, a0, k_k, k_a, r_k, *loras, bd)


def _rowmul_bf16(xb, row):
    pack = 2 * SUBLANES
    rb = jnp.broadcast_to(row, (pack, LANES)).astype(BF16)
    return (xb.reshape(xb.shape[0] // pack, pack, LANES) * rb[None]).reshape(xb.shape)


def _rwkv_scan_kernel(w_ref, r_ref, a_ref, b_ref, kx_ref, v_ref, s0_ref, bd_ref, ipat_ref,
                      y_ref, s_ref, lhs_ref, res_ref, *, n_group, n_tok):
    @pl.when(pl.program_id(1) == 0)
    def _():
        s_ref[...] = s0_ref[...]

    ipat = ipat_ref[...]
    ipat_b = ipat.astype(BF16)
    seg = 3 * RW_HD
    blocks = [(g, c) for g in range(n_group) for c in range(RW_LANE_TILES)]
    slot = lambda g, c: g * (RW_LANE_TILES // 2) + c // 2
    lanes_of = lambda c: slice((c % 2) * LANES, (c % 2 + 1) * LANES)

    def put_y(g, t_out, c, spread):
        y_ref[g, t_out, pl.ds(c, 1), :] = jnp.sum(spread * ipat, axis=0, keepdims=True)

    def step(t, carry):
        tp = jnp.maximum(t - 1, 0)
        for g in range(n_group):
            a, rp, vv = a_ref[g, t], r_ref[g, tp], v_ref[g, t]
            w, b, kx = w_ref[g, t], b_ref[g, t], kx_ref[g, t]
            for c0 in range(0, RW_LANE_TILES, SCAN_CHUNK):
                lo = slot(g, c0) * seg
                hi = lo + SCAN_CHUNK // 2 * seg
                for c in range(c0, c0 + SCAN_CHUNK):
                    sb = s_ref[g, c].astype(BF16)
                    base, cols = slot(g, c) * seg, lanes_of(c)
                    lhs_ref[base:base + RW_HD, cols] = _rowmul_bf16(sb, a[c:c + 1])
                    lhs_ref[base + RW_HD:base + 2 * RW_HD, cols] = _rowmul_bf16(sb, rp[c:c + 1])
                    lhs_ref[base + 2 * RW_HD:base + seg, cols] = _rowmul_bf16(ipat_b, vv[c:c + 1])
                res_ref[lo:hi, :] = _dot(lhs_ref[lo:hi, :], bd_ref[...])
                for c in range(c0, c0 + SCAN_CHUNK):
                    base, cols = slot(g, c) * seg, lanes_of(c)
                    sa = res_ref[base:base + RW_HD, cols]
                    vx = res_ref[base + 2 * RW_HD:base + seg, cols]
                    put_y(g, tp, c, res_ref[base + RW_HD:base + 2 * RW_HD, cols])
                    s_ref[g, c] = s_ref[g, c] * w[c:c + 1] + sa * b[c:c + 1] + vx * kx[c:c + 1]
        return carry

    lax.fori_loop(0, n_tok, step, 0)

    n_rows = len(blocks) // 2 * RW_HD
    for g, c in blocks:
        base = slot(g, c) * RW_HD
        lhs_ref[base:base + RW_HD, lanes_of(c)] = _rowmul_bf16(s_ref[g, c].astype(BF16),
                                                              r_ref[g, n_tok - 1][c:c + 1])
    res_ref[0:n_rows, :] = _dot(lhs_ref[0:n_rows, :], bd_ref[...])
    for g, c in blocks:
        base = slot(g, c) * RW_HD
        put_y(g, n_tok - 1, c, res_ref[base:base + RW_HD, lanes_of(c)])


def rwkv_scan(seq_inputs, s0, bd_wide, ipat, n_group, n_tok):
    n, t = seq_inputs[0].shape[:2]
    tok = pl.BlockSpec((n_group, n_tok, RW_LANE_TILES, LANES), lambda i, j: (i, j, 0, 0))
    st = pl.BlockSpec((n_group, RW_LANE_TILES, RW_HD, LANES), lambda i, j: (i, 0, 0, 0))
    lhs_rows = n_group * (RW_LANE_TILES // 2) * 3 * RW_HD
    wide = 2 * LANES
    kern = functools.partial(_rwkv_scan_kernel, n_group=n_group, n_tok=n_tok)
    tok_bytes = _nbytes((n_group, n_tok, RW_LANE_TILES, LANES), F32)
    st_bytes = _nbytes((n_group, RW_LANE_TILES, RW_HD, LANES), F32)
    return pl.pallas_call(
        kern,
        out_shape=[jax.ShapeDtypeStruct((n, t, RW_LANE_TILES, LANES), F32),
                   jax.ShapeDtypeStruct(s0.shape, F32)],
        grid=(n // n_group, t // n_tok),
        in_specs=[tok] * 6 + [st, pl.BlockSpec((wide, wide), lambda i, j: (0, 0)),
                              pl.BlockSpec((RW_HD, LANES), lambda i, j: (0, 0))],
        out_specs=[tok, st],
        scratch_shapes=[pltpu.VMEM((lhs_rows, wide), BF16), pltpu.VMEM((lhs_rows, wide), F32)],
        compiler_params=pltpu.CompilerParams(
            dimension_semantics=("arbitrary", "arbitrary"),
            vmem_limit_bytes=_vmem_limit(14 * tok_bytes, 4 * st_bytes,
                                         _nbytes((lhs_rows, wide), BF16),
                                         _nbytes((lhs_rows, wide), F32))),
        name="rwkv_scan",
    )(*seq_inputs, s0, bd_wide, ipat)


def _rwkv_post_kernel(y_ref, g_ref, bonus_ref, lw_ref, lb_ref, bd_ref, o_ref):
    y = y_ref[...]
    bd = bd_ref[...]
    mu = _segsum(y, bd) * (1.0 / RW_HD)
    d = y - mu
    var = _segsum(d * d, bd) * (1.0 / RW_HD)
    yn = d * lax.rsqrt(var + LNX_EPS) * lw_ref[...] + lb_ref[...]
    o_ref[...] = ((yn + bonus_ref[...]) * g_ref[...]).astype(o_ref.dtype)


def rwkv_post(y, g, bonus, lnx_w, lnx_b, bd, tb):
    t = y.shape[0]
    row = pl.BlockSpec((tb, RW_W), lambda i: (i, 0))
    const = lambda shape: pl.BlockSpec(shape, lambda i: (0, 0))
    return pl.pallas_call(
        _rwkv_post_kernel,
        out_shape=jax.ShapeDtypeStruct((t, RW_W), BF16),
        grid=(t // tb,),
        in_specs=[row, row, row, const((1, RW_W)), const((1, RW_W)), const((LANES, LANES))],
        out_specs=row,
        compiler_params=pltpu.CompilerParams(
            dimension_semantics=("arbitrary",),
            vmem_limit_bytes=_vmem_limit(16 * _nbytes((tb, RW_W), F32))),
        name="rwkv_post",
    )(y, g, bonus, lnx_w, lnx_b, bd)


def _top_mask(score, index, n_pick, axis):
    index = index.astype(F32)
    sel = jnp.zeros(score.shape, F32)
    for _ in range(n_pick):
        mx = jnp.max(score, axis=axis, keepdims=True)
        first = jnp.min(jnp.where(score == mx, index, BIG_INDEX), axis=axis, keepdims=True)
        hit = index == first
        sel = jnp.maximum(sel, jnp.where(hit, jnp.where(mx > 0.5 * NEG, 1.0, 0.0), 0.0))
        score = jnp.where(hit, NEG, score)
    return sel


def _moba_prompt_kernel(q_ref, k_ref, v_ref, bias_ref, expand_ref, o_ref, kmean_ref, m_ref, l_ref,
                        acc_ref, *, n_blocks, n_bias):
    i = pl.program_id(2)
    bs = MOBA_BLOCK
    nt = (((1,), (1,)), ((), ()))

    @pl.when(i == 0)
    def _():
        kmean_ref[...] = jnp.zeros(kmean_ref.shape, F32)
        kmean_ref[0:n_blocks, :] = jnp.mean(k_ref[...].reshape(n_blocks, bs, ATT_HD), axis=1)

    q = q_ref[...]
    km1, km2, km3 = _split3(kmean_ref[...])
    score = (lax.dot_general(q, km1, nt, preferred_element_type=F32)
             + lax.dot_general(q, km2, nt, preferred_element_type=F32)
             + lax.dot_general(q, km3, nt, preferred_element_type=F32))
    lane = lax.broadcasted_iota(I32, score.shape, 1)
    sel = _top_mask(jnp.where(lane < i, score, NEG), lane, MOBA_TOPK, axis=1)
    attend = jnp.where(lane == i, 1.0, sel).astype(BF16)

    m_ref[...] = jnp.full(m_ref.shape, NEG, F32)
    l_ref[...] = jnp.zeros(l_ref.shape, F32)
    acc_ref[...] = jnp.zeros(acc_ref.shape, F32)
    gw = MOBA_GROUP * bs

    for g in range(n_blocks // MOBA_GROUP):
        @pl.when(g * MOBA_GROUP <= i)
        def _(g=g):
            rows = slice(g * gw, (g + 1) * gw)
            s = lax.dot_general(q, k_ref[rows, :].astype(BF16), nt, preferred_element_type=F32) * ATT_SCALE
            bias = [bias_ref[jnp.clip(i - (g * MOBA_GROUP + u), 0, n_bias - 1)] for u in range(MOBA_GROUP)]
            on = _dot(attend, expand_ref[g])
            s = jnp.where(on > 0.5, s + jnp.concatenate(bias, axis=-1), NEG)
            m_old = m_ref[...]
            m_new = jnp.maximum(m_old, jnp.max(s, axis=-1, keepdims=True))
            alpha = jnp.exp(m_old - m_new)
            p = jnp.exp(s - m_new)
            l_ref[...] = alpha * l_ref[...] + jnp.sum(p, axis=-1, keepdims=True)
            acc_ref[...] = alpha * acc_ref[...] + _dot(p.astype(BF16), v_ref[rows, :].astype(BF16))
            m_ref[...] = m_new

    o_ref[...] = (acc_ref[...] / l_ref[...]).astype(o_ref.dtype)


def moba_prompt(q, k, v, bias_tiles, n, t):
    bs = MOBA_BLOCK
    n_blocks = t // bs
    n_groups = n_blocks // MOBA_GROUP
    n_bias = bias_tiles.shape[1]
    gw = MOBA_GROUP * bs
    blk_of_key = jnp.arange(n_groups * gw, dtype=I32).reshape(n_groups, 1, gw) // bs
    expand = (jnp.arange(LANES, dtype=I32)[None, :, None] == blk_of_key).astype(BF16)
    qo = pl.BlockSpec((bs, ATT_HD), lambda s, h, i: (s * n_blocks + i, h))
    kv = pl.BlockSpec((t, ATT_HD), lambda s, h, i: (s, h))
    kern = functools.partial(_moba_prompt_kernel, n_blocks=n_blocks, n_bias=n_bias)
    return pl.pallas_call(
        kern,
        out_shape=jax.ShapeDtypeStruct((n * t, ATT_W), BF16),
        grid=(n, ATT_HEADS, n_blocks),
        in_specs=[qo, kv, kv,
                  pl.BlockSpec((None, n_bias, bs, bs), lambda s, h, i: (h, 0, 0, 0)),
                  pl.BlockSpec((n_groups, LANES, gw), lambda s, h, i: (0, 0, 0))],
        out_specs=qo,
        scratch_shapes=[pltpu.VMEM((LANES, ATT_HD), F32), pltpu.VMEM((bs, 1), F32),
                        pltpu.VMEM((bs, 1), F32), pltpu.VMEM((bs, ATT_HD), F32)],
        compiler_params=pltpu.CompilerParams(
            dimension_semantics=("arbitrary", "arbitrary", "arbitrary"),
            vmem_limit_bytes=_vmem_limit(4 * _nbytes((t, ATT_HD), F32),
                                         2 * _nbytes((n_bias, bs, bs), F32),
                                         2 * _nbytes((n_groups, LANES, gw), BF16),
                                         12 * _nbytes((bs, gw), F32))),
        name="moba_prompt",
    )(q, k, v, bias_tiles, expand)


def _moba_sample_kernel(pt_ref, qall_ref, *refs, n_pages, pps, n_new):
    del pt_ref
    kp_refs, vp_refs = refs[:pps], refs[pps:2 * pps]
    kn_ref, vn_ref, bias_ref, o_ref, r_ref, lg_ref, p_ref, acc_ref = refs[2 * pps:]
    j = pl.program_id(1)
    nh = ATT_HEADS
    page = kp_refs[0].shape[0] // nh
    past = n_pages * page
    n_ksteps = n_pages // pps
    n_blocks = past // MOBA_BLOCK
    rows_all = lg_ref.shape[0]
    lane = lax.broadcasted_iota(I32, (1, LANES), 1)
    head_of_lane = lane >> (n_new.bit_length() - 1)
    head_cols = [jnp.where(head_of_lane == h, 1.0, 0.0) for h in range(nh)]

    def raw_logits(src_ref, n_tok):
        r_ref[0:n_tok * nh, :] = _dot(src_ref[...].astype(BF16), qall_ref[...])
        out = r_ref[pl.ds(0, n_tok, stride=nh), :] * head_cols[0]
        for h in range(1, nh):
            out = out + r_ref[pl.ds(h, n_tok, stride=nh), :] * head_cols[h]
        return out

    @pl.when(j < n_ksteps)
    def _():
        for u in range(pps):
            row0 = pl.multiple_of((j * pps + u) * page, page)
            lg_ref[pl.ds(row0, page), :] = raw_logits(kp_refs[u], page)

    @pl.when(j == n_ksteps - 1)
    def _():
        new = raw_logits(kn_ref, n_new) * ATT_SCALE + bias_ref[past:past + n_new, :]
        jn = lax.broadcasted_iota(I32, new.shape, 0)
        qi = lax.broadcasted_iota(I32, new.shape, 1) & (n_new - 1)
        score = jnp.mean(lg_ref[0:past, :].reshape(n_blocks, MOBA_BLOCK, LANES), axis=1)
        sub = lax.broadcasted_iota(I32, score.shape, 0)
        sel = _top_mask(score, sub, MOBA_TOPK, axis=0)
        for kb in range(n_blocks):
            rows = slice(kb * MOBA_BLOCK, (kb + 1) * MOBA_BLOCK)
            s = lg_ref[rows, :] * ATT_SCALE + bias_ref[rows, :]
            lg_ref[rows, :] = jnp.where(sel[kb:kb + 1, :] > 0.5, s, NEG)
        lg_ref[past:past + n_new, :] = jnp.where(jn <= qi, new, NEG)
        lg_ref[past + n_new:rows_all, :] = jnp.full((rows_all - past - n_new, LANES), NEG, F32)
        lg = lg_ref[...]
        p = jnp.exp(lg - jnp.max(lg, axis=0, keepdims=True))
        p_ref[...] = p / jnp.sum(p, axis=0, keepdims=True)

    @pl.when(j == n_ksteps)
    def _():
        acc_ref[...] = jnp.zeros(acc_ref.shape, F32)

    def add_pv(p_rows, v_of_head):
        pt = p_rows.T.astype(BF16)
        for h in range(nh):
            lo = (h // 2) * 2 * n_new
            o = _dot(pt[lo:lo + 2 * n_new, :], v_of_head(h))
            off = (h % 2) * n_new
            acc_ref[h * n_new:(h + 1) * n_new, :] += o[off:off + n_new, :]

    @pl.when(j >= n_ksteps)
    def _():
        for u in range(pps):
            row0 = pl.multiple_of(((j - n_ksteps) * pps + u) * page, page)
            add_pv(p_ref[pl.ds(row0, page), :],
                   lambda h, u=u: vp_refs[u][pl.ds(h, page, stride=nh), :].astype(BF16))

    @pl.when(j == 2 * n_ksteps - 1)
    def _():
        def v_new(h):
            rows = vn_ref[pl.ds(h, n_new, stride=nh), :]
            return jnp.concatenate([rows, jnp.zeros((rows_all - past - n_new, ATT_HD), F32)],
                                   axis=0).astype(BF16)

        add_pv(p_ref[past:rows_all, :], v_new)
        acc = acc_ref[...]
        for h in range(nh):
            o_ref[:, h * ATT_HD:(h + 1) * ATT_HD] = acc[h * n_new:(h + 1) * n_new, :]


def moba_sample(q_all, cache_k, cache_v, k_new, v_new, page_table, bias_t):
    n_seq, n_pages = page_table.shape
    pps = SAMPLE_PAGES_PER_STEP
    page_rows = cache_k.shape[1]
    n_new = LANES // ATT_HEADS
    past = n_pages * page_rows // ATT_HEADS
    rows_all = past + LANES
    n_ksteps = n_pages // pps
    kern = functools.partial(_moba_sample_kernel, n_pages=n_pages, pps=pps, n_new=n_new)

    def page_spec(u, for_v):
        def index(s, j, pt):
            step = jnp.maximum(j - n_ksteps, 0) if for_v else jnp.minimum(j, n_ksteps - 1)
            return (pt[s * n_pages + step * pps + u], 0, 0)
        return pl.BlockSpec((None, page_rows, ATT_HD), index)

    per_seq = lambda rows, cols: pl.BlockSpec((None, rows, cols), lambda s, j, pt: (s, 0, 0))
    grid_spec = pltpu.PrefetchScalarGridSpec(
        num_scalar_prefetch=1,
        grid=(n_seq, 2 * n_ksteps),
        in_specs=[per_seq(ATT_HD, LANES)]
                 + [page_spec(u, False) for u in range(pps)] + [page_spec(u, True) for u in range(pps)]
                 + [per_seq(n_new * ATT_HEADS, ATT_HD), per_seq(n_new * ATT_HEADS, ATT_HD),
                    pl.BlockSpec((rows_all, LANES), lambda s, j, pt: (0, 0))],
        out_specs=per_seq(n_new, ATT_W),
        scratch_shapes=[pltpu.VMEM((page_rows, LANES), F32), pltpu.VMEM((rows_all, LANES), F32),
                        pltpu.VMEM((rows_all, LANES), F32), pltpu.VMEM((LANES, ATT_HD), F32)],
    )
    return pl.pallas_call(
        kern,
        out_shape=jax.ShapeDtypeStruct((n_seq, n_new, ATT_W), F32),
        grid_spec=grid_spec,
        compiler_params=pltpu.CompilerParams(
            dimension_semantics=("arbitrary", "arbitrary"),
            vmem_limit_bytes=_vmem_limit((4 * pps + 4) * _nbytes((page_rows, ATT_HD), F32),
                                         8 * _nbytes((rows_all, LANES), F32))),
        name="moba_sample",
    )(page_table.reshape(-1), q_all, *([cache_k] * pps), *([cache_v] * pps), k_new, v_new, bias_t)


def _router_kernel(x_ref, g_ref, rw_ref, rb_ref, tril_ref, hn_ref, gate_ref, e_ref, rank_ref,
                   cnt_ref, run_ref):
    @pl.when(pl.program_id(0) == 0)
    def _():
        run_ref[...] = jnp.zeros(run_ref.shape, F32)

    x = x_ref[...]
    hn = x * lax.rsqrt(jnp.mean(x * x, axis=-1, keepdims=True) + RMS_EPS) * g_ref[...]
    hn_ref[...] = hn
    a1, a2, a3 = _split3(hn)
    b1, b2, b3 = _split3(rw_ref[...])
    logits = (_dot(a1, b1) + _dot(a1, b2) + _dot(a2, b1) + _dot(a2, b2) + _dot(a1, b3)
              + _dot(a3, b1)) + rb_ref[...]
    lane = lax.broadcasted_iota(I32, logits.shape, 1).astype(F32)
    onehot = jnp.zeros(logits.shape, F32)
    vals, idxs = [], []
    s = logits
    for _ in range(TOP_K):
        mx = jnp.max(s, axis=-1, keepdims=True)
        first = jnp.min(jnp.where(s == mx, lane, BIG_INDEX), axis=-1, keepdims=True)
        hit = lane == first
        onehot = jnp.where(hit, 1.0, onehot)
        s = jnp.where(hit, NEG, s)
        vals.append(mx)
        idxs.append(first)
    ex = [jnp.exp(v - vals[0]) for v in vals]
    den = ex[0] + ex[1] + ex[2] + ex[3]
    before = _dot(tril_ref[...], onehot.astype(BF16)) + run_ref[0:1, :]
    gate = jnp.zeros(logits.shape, F32)
    e_out = jnp.zeros(logits.shape, I32)
    rank = jnp.zeros(logits.shape, I32)
    for kk in range(TOP_K):
        slot = lane == kk
        rk = jnp.sum(jnp.where(lane == idxs[kk], before, 0.0), axis=-1, keepdims=True)
        gate = jnp.where(slot, ex[kk] / den, gate)
        e_out = jnp.where(slot, idxs[kk].astype(I32), e_out)
        rank = jnp.where(slot, rk.astype(I32), rank)
    gate_ref[...] = gate
    e_ref[...] = e_out
    rank_ref[...] = rank
    run_ref[...] = run_ref[...] + jnp.sum(onehot, axis=0, keepdims=True)
    cnt_ref[...] = run_ref[...]


def router(x, norm_g, router_w_pad, router_b_pad, tb):
    t, d = x.shape
    tril = (lax.broadcasted_iota(I32, (tb, tb), 0) > lax.broadcasted_iota(I32, (tb, tb), 1)).astype(BF16)
    row = lambda width: pl.BlockSpec((tb, width), lambda i: (i, 0))
    const = lambda shape: pl.BlockSpec(shape, lambda i: (0, 0))
    return pl.pallas_call(
        _router_kernel,
        out_shape=[jax.ShapeDtypeStruct((t, d), F32), jax.ShapeDtypeStruct((t, LANES), F32),
                   jax.ShapeDtypeStruct((t, LANES), I32), jax.ShapeDtypeStruct((t, LANES), I32),
                   jax.ShapeDtypeStruct((SUBLANES, LANES), F32)],
        grid=(t // tb,),
        in_specs=[row(d), const((1, d)), const((d, LANES)), const((1, LANES)), const((tb, tb))],
        out_specs=[row(d), row(LANES), row(LANES), row(LANES), const((SUBLANES, LANES))],
        scratch_shapes=[pltpu.VMEM((SUBLANES, LANES), F32)],
        compiler_params=pltpu.CompilerParams(
            dimension_semantics=("arbitrary",),
            vmem_limit_bytes=_vmem_limit(10 * _nbytes((tb, d), F32), 8 * _nbytes((d, LANES), F32))),
        name="router",
    )(x, norm_g.reshape(1, d), router_w_pad, router_b_pad, tril)


def _row_copy(src_hbm, buf, sem, src_row, dst_row):
    return pltpu.make_async_copy(src_hbm.at[pl.ds(src_row, 1), :], buf.at[pl.ds(dst_row, 1), :], sem)


def _gather_rows_kernel(idx_ref, nxt_ref, src_hbm, o_ref, buf, sem):
    i = pl.program_id(0)
    n = buf.shape[1]
    slot = i % 2

    def issue(ids_ref, s):
        def body(r, c):
            _row_copy(src_hbm, buf.at[s], sem.at[s], ids_ref[0, 0, r], r).start()
            return c
        lax.fori_loop(0, n, body, 0)

    @pl.when(i == 0)
    def _():
        issue(idx_ref, 0)

    @pl.when(i + 1 < pl.num_programs(0))
    def _():
        issue(nxt_ref, 1 - slot)

    def drain(r, c):
        _row_copy(src_hbm, buf.at[slot], sem.at[slot], 0, r).wait()
        return c

    lax.fori_loop(0, n, drain, 0)
    o_ref[...] = buf[slot].astype(o_ref.dtype)


def gather_rows(src, row_idx, tm, out_dtype):
    n_rows = row_idx.shape[0]
    d = src.shape[1]
    n_steps = n_rows // tm
    ids = row_idx.reshape(n_steps, 1, tm)
    return pl.pallas_call(
        _gather_rows_kernel,
        out_shape=jax.ShapeDtypeStruct((n_rows, d), out_dtype),
        grid=(n_steps,),
        in_specs=[pl.BlockSpec((1, 1, tm), lambda i: (i, 0, 0), memory_space=pltpu.SMEM),
                  pl.BlockSpec((1, 1, tm), lambda i: (jnp.minimum(i + 1, n_steps - 1), 0, 0),
                               memory_space=pltpu.SMEM),
                  pl.BlockSpec(memory_space=pl.ANY)],
        out_specs=pl.BlockSpec((tm, d), lambda i: (i, 0)),
        scratch_shapes=[pltpu.VMEM((2, tm, d), src.dtype), pltpu.SemaphoreType.DMA((2,))],
        compiler_params=pltpu.CompilerParams(
            dimension_semantics=("arbitrary",),
            vmem_limit_bytes=_vmem_limit(3 * _nbytes((tm, d), F32), 2 * _nbytes((tm, d), out_dtype))),
        name="gather_rows",
    )(ids, ids, src)


def _combine_kernel(idx_ref, y_hbm, x_ref, gate_ref, o_ref, buf, sem, *, tb):
    n = buf.shape[0]

    def issue(r, c):
        _row_copy(y_hbm, buf, sem, idx_ref[0, 0, r], r).start()
        return c

    def drain(r, c):
        _row_copy(y_hbm, buf, sem, 0, r).wait()
        return c

    lax.fori_loop(0, n, issue, 0)
    lax.fori_loop(0, n, drain, 0)
    gate = gate_ref[...]
    ffn = gate[:, 0:1] * buf[0:tb, :]
    for kk in range(1, TOP_K):
        ffn = ffn + gate[:, kk:kk + 1] * buf[kk * tb:(kk + 1) * tb, :]
    o_ref[...] = x_ref[...] + ffn


def moe_combine(yb, x, gate, dest, tb):
    t, d = x.shape
    idx = dest.reshape(t // tb, tb, TOP_K).transpose(0, 2, 1).reshape(t // tb, 1, TOP_K * tb)
    kern = functools.partial(_combine_kernel, tb=tb)
    return pl.pallas_call(
        kern,
        out_shape=jax.ShapeDtypeStruct((t, d), F32),
        grid=(t // tb,),
        in_specs=[pl.BlockSpec((1, 1, TOP_K * tb), lambda i: (i, 0, 0), memory_space=pltpu.SMEM),
                  pl.BlockSpec(memory_space=pl.ANY),
                  pl.BlockSpec((tb, d), lambda i: (i, 0)),
                  pl.BlockSpec((tb, LANES), lambda i: (i, 0))],
        out_specs=pl.BlockSpec((tb, d), lambda i: (i, 0)),
        scratch_shapes=[pltpu.VMEM((TOP_K * tb, d), F32), pltpu.SemaphoreType.DMA],
        compiler_params=pltpu.CompilerParams(
            dimension_semantics=("arbitrary",),
            vmem_limit_bytes=_vmem_limit((TOP_K + 6) * _nbytes((tb, d), F32))),
        name="moe_combine",
    )(idx, yb, x, gate)


def _first_of_expert(be_ref, i):
    return (i == 0) | (be_ref[i] != be_ref[jnp.maximum(i - 1, 0)])


def _moe_gu_kernel(be_ref, nu_ref, x_ref, wg_ref, wu_ref, bg_ref, bu_ref, o_ref, wgb_ref, wub_ref):
    i = pl.program_id(1)
    used = i < nu_ref[0]

    @pl.when(used & _first_of_expert(be_ref, i))
    def _():
        wgb_ref[...] = wg_ref[...].astype(BF16)
        wub_ref[...] = wu_ref[...].astype(BF16)

    @pl.when(used)
    def _():
        x = x_ref[...]
        g = jnp.minimum(_dot(x, wgb_ref[...]) + bg_ref[...], SWIGLU_LIMIT)
        u = jnp.clip(_dot(x, wub_ref[...]) + bu_ref[...], -SWIGLU_LIMIT, SWIGLU_LIMIT)
        o_ref[...] = ((u + 1.0) * (g * jax.nn.sigmoid(SWIGLU_ALPHA * g))).astype(o_ref.dtype)

    @pl.when(jnp.logical_not(used))
    def _():
        o_ref[...] = jnp.zeros(o_ref.shape, o_ref.dtype)


def moe_gate_up(xb, w_gu, b_gu, block_e, n_used, n_blocks):
    n_rows, d = xb.shape
    tm, tn = MOE_TM, MOE_TN_GU
    nj = D_FF // tn
    blk = lambda i, nu: jnp.minimum(i, nu[0] - 1)
    grid_spec = pltpu.PrefetchScalarGridSpec(
        num_scalar_prefetch=2,
        grid=(nj, n_blocks),
        in_specs=[
            pl.BlockSpec((tm, d), lambda j, i, be, nu: (blk(i, nu), 0)),
            pl.BlockSpec((None, d, tn), lambda j, i, be, nu: (be[blk(i, nu)], 0, j)),
            pl.BlockSpec((None, d, tn), lambda j, i, be, nu: (be[blk(i, nu)], 0, nj + j)),
            pl.BlockSpec((None, 1, tn), lambda j, i, be, nu: (be[blk(i, nu)], 0, j)),
            pl.BlockSpec((None, 1, tn), lambda j, i, be, nu: (be[blk(i, nu)], 0, nj + j)),
        ],
        out_specs=pl.BlockSpec((tm, tn), lambda j, i, be, nu: (i, j)),
        scratch_shapes=[pltpu.VMEM((d, tn), BF16), pltpu.VMEM((d, tn), BF16)],
    )
    return pl.pallas_call(
        _moe_gu_kernel,
        out_shape=jax.ShapeDtypeStruct((n_rows, D_FF), BF16),
        grid_spec=grid_spec,
        compiler_params=pltpu.CompilerParams(
            dimension_semantics=("arbitrary", "arbitrary"),
            vmem_limit_bytes=_vmem_limit(2 * _nbytes((tm, d), BF16), 4 * _nbytes((d, tn), F32),
                                         2 * _nbytes((d, tn), BF16), 8 * _nbytes((tm, tn), F32))),
        name="moe_gate_up",
    )(block_e, n_used, xb, w_gu, w_gu, b_gu.reshape(N_EXPERTS, 1, 2 * D_FF),
      b_gu.reshape(N_EXPERTS, 1, 2 * D_FF))


def _moe_down_kernel(be_ref, nu_ref, a_ref, w_ref, b_ref, o_ref, wb_ref):
    i = pl.program_id(1)
    used = i < nu_ref[0]

    @pl.when(used & _first_of_expert(be_ref, i))
    def _():
        wb_ref[...] = w_ref[...].astype(BF16)

    @pl.when(used)
    def _():
        o_ref[...] = _dot(a_ref[...], wb_ref[...]) + b_ref[...]

    @pl.when(jnp.logical_not(used))
    def _():
        o_ref[...] = jnp.zeros(o_ref.shape, o_ref.dtype)


def moe_down(act, w_down, b_down, block_e, n_used, n_blocks):
    n_rows, dff = act.shape
    tm, tn = MOE_TM, MOE_TN_DOWN
    blk = lambda i, nu: jnp.minimum(i, nu[0] - 1)
    grid_spec = pltpu.PrefetchScalarGridSpec(
        num_scalar_prefetch=2,
        grid=(D_MODEL // tn, n_blocks),
        in_specs=[
            pl.BlockSpec((tm, dff), lambda j, i, be, nu: (blk(i, nu), 0)),
            pl.BlockSpec((None, dff, tn), lambda j, i, be, nu: (be[blk(i, nu)], 0, j)),
            pl.BlockSpec((None, 1, tn), lambda j, i, be, nu: (be[blk(i, nu)], 0, j)),
        ],
        out_specs=pl.BlockSpec((tm, tn), lambda j, i, be, nu: (i, j)),
        scratch_shapes=[pltpu.VMEM((dff, tn), BF16)],
    )
    return pl.pallas_call(
        _moe_down_kernel,
        out_shape=jax.ShapeDtypeStruct((n_rows, D_MODEL), F32),
        grid_spec=grid_spec,
        compiler_params=pltpu.CompilerParams(
            dimension_semantics=("arbitrary", "arbitrary"),
            vmem_limit_bytes=_vmem_limit(2 * _nbytes((tm, dff), BF16), 2 * _nbytes((dff, tn), F32),
                                         _nbytes((dff, tn), BF16), 4 * _nbytes((tm, tn), F32))),
        name="moe_down",
    )(block_e, n_used, act, w_down, b_down.reshape(N_EXPERTS, 1, D_MODEL))


def _t5_bucket(dist):
    n = jnp.maximum(dist, 0)
    max_exact = N_BUCKETS // 2
    nf = jnp.maximum(n, 1).astype(F32)
    large = max_exact + (jnp.log(nf / max_exact) / math.log(MAX_DISTANCE / max_exact)
                         * (N_BUCKETS - max_exact)).astype(I32)
    return jnp.where(n < max_exact, n, jnp.minimum(large, N_BUCKETS - 1))


def _bias_lookup(rel_bias, dist):
    onehot = (_t5_bucket(dist)[..., None] == jnp.arange(N_BUCKETS, dtype=I32)).astype(F32)
    return jnp.einsum("...b,bh->...h", onehot, rel_bias.astype(F32), precision=lax.Precision.HIGHEST)


def _state_to_tiles(s):
    n = s.shape[0]
    s = s.reshape(n, RW_LANE_TILES, LANES // RW_HD, RW_HD, RW_HD).transpose(0, 1, 3, 2, 4)
    return s.reshape(n, RW_LANE_TILES, RW_HD, LANES)


def _tiles_to_state(s):
    n = s.shape[0]
    s = s.reshape(n, RW_LANE_TILES, RW_HD, LANES // RW_HD, RW_HD).transpose(0, 1, 3, 2, 4)
    return s.reshape(n, RW_HEADS, RW_HD, RW_HD)


def _pad_rows(w, row0, n_rows):
    return jnp.zeros((n_rows, w.shape[1]), BF16).at[row0:row0 + w.shape[0]].set(w.astype(BF16))


def kernel(x_prompt, x_sample, cache_k, cache_v, page_table, state_rwkv, state_shift, norm1_g, w_in,
           q_norm_g, k_norm_g, rel_bias, mu_shift, w0, w_lora_up, a0, a_lora_up, g_lora_up, k_k, k_a,
           r_k, lnx_w, lnx_b, w_branch_a, w_branch_b, w_out, norm2_g, router_w, router_b, w_gu, b_gu,
           w_down, b_down):
    n_p, t_p, d = x_prompt.shape
    n_s, t_s, _ = x_sample.shape
    tok_p, tok_s = n_p * t_p, n_s * t_s
    n_tok = tok_p + tok_s
    x = jnp.concatenate([x_prompt.reshape(tok_p, d), x_sample.reshape(tok_s, d)], axis=0)

    h = rmsnorm_rows(x, norm1_g, BF16, 256)
    cb = ATT_W // MM_TN
    p_q = matmul(h, w_in, ATT_W, col_block0=0, name="proj_q")
    p_k = matmul(h, w_in, ATT_W, col_block0=cb, name="proj_k")
    p_v = matmul(h, w_in, ATT_W, col_block0=2 * cb, name="proj_v")
    p_rw = matmul(h, w_in, RW_PAD, col_block0=3 * cb, name="proj_rw")
    w_gate = w_in[:, 3 * ATT_W + RW_COLS:]
    p_ga = matmul(h, w_gate, D_MODEL, col_block0=0, name="proj_gate_a")
    p_gb = matmul(h, w_gate, D_MODEL, col_block0=D_MODEL // MM_TN, name="proj_gate_b")

    head_rows = n_tok * ATT_HEADS
    q_n = rmsnorm_rows(p_q.reshape(head_rows, ATT_HD), q_norm_g, BF16, 2048).reshape(n_tok, ATT_W)
    k_n = rmsnorm_rows(p_k.reshape(head_rows, ATT_HD), k_norm_g, F32, 2048).reshape(n_tok, ATT_W)

    past = page_table.shape[1] * cache_k.shape[1]
    n_bias = 6
    ri = jnp.arange(MOBA_BLOCK, dtype=I32)
    dist = (jnp.arange(n_bias, dtype=I32)[:, None, None] * MOBA_BLOCK + ri[None, :, None] - ri[None, None, :])
    bias_tiles = _bias_lookup(rel_bias, dist).transpose(3, 0, 1, 2)
    bias_tiles = jnp.where(dist[None] >= 0, bias_tiles, NEG)
    o_att_p = moba_prompt(q_n, k_n, p_v, bias_tiles, n_p, t_p)

    q_s = q_n[tok_p:].reshape(n_s, t_s, ATT_HEADS, ATT_HD)
    q_all = q_s.transpose(0, 3, 2, 1).reshape(n_s, ATT_HD, ATT_HEADS * t_s)
    new_rows = lambda a: a[tok_p:].reshape(n_s, t_s * ATT_HEADS, ATT_HD)
    key_pos = jnp.arange(past + LANES, dtype=I32)
    q_idx = jnp.arange(t_s, dtype=I32)
    d_s = past + q_idx[None, :] - key_pos[:, None]
    bias_t = _bias_lookup(rel_bias, d_s).transpose(0, 2, 1).reshape(past + LANES, ATT_HEADS * t_s)
    pool = cache_k.shape[0]
    o_att_s = moba_sample(q_all, cache_k.reshape(pool, -1, ATT_HD), cache_v.reshape(pool, -1, ATT_HD),
                          new_rows(k_n), new_rows(p_v), page_table, bias_t)
    o_att = jnp.concatenate([o_att_p, o_att_s.reshape(tok_s, ATT_W).astype(BF16)], axis=0)

    shift_s = jnp.pad(state_shift.astype(F32), ((0, 0), (0, RW_PAD - RW_COLS)))
    row = lambda a: a.reshape(1, -1).astype(F32)
    vecs = (jnp.pad(row(mu_shift), ((0, 0), (0, RW_PAD - RW_COLS))), row(w0), row(a0), row(k_k), row(k_a),
            row(r_k))
    loras = (_pad_rows(w_lora_up, 0, LORA_PAD), _pad_rows(a_lora_up, W_LORA, LORA_PAD),
             _pad_rows(g_lora_up, W_LORA + A_LORA, LORA_PAD))
    lane_i = jnp.arange(LANES, dtype=I32)
    bd = (lane_i[:, None] // RW_HD == lane_i[None, :] // RW_HD).astype(BF16)
    wide_i = jnp.arange(2 * LANES, dtype=I32)
    bd_wide = (wide_i[:, None] // RW_HD == wide_i[None, :] // RW_HD).astype(BF16)
    ipat = (jnp.arange(RW_HD, dtype=I32)[:, None] == lane_i[None, :] % RW_HD).astype(F32)
    def rwkv_group(row0, n, t, shift, s0, n_group, n_tok):
        prep = rwkv_prep(p_rw, row0, n, t, shift, vecs, loras, bd)
        tiles = lambda a: a.reshape(n, t, RW_LANE_TILES, LANES)
        y, s = rwkv_scan([tiles(a) for a in prep[:6]], s0, bd_wide, ipat, n_group, n_tok)
        o = rwkv_post(y.reshape(n * t, RW_W), prep[6], prep[7], row(lnx_w), row(lnx_b), bd, 256)
        return o, s

    o_rw_p, s_p = rwkv_group(0, n_p, t_p, jnp.zeros((n_p, RW_PAD), F32),
                             jnp.zeros((n_p, RW_LANE_TILES, RW_HD, LANES), F32), n_p, 64)
    o_rw_s, s_s = rwkv_group(tok_p, n_s, t_s, shift_s, _state_to_tiles(state_rwkv.astype(F32)), 4, t_s)
    o_rw = jnp.concatenate([o_rw_p, o_rw_s], axis=0)

    m_a = matmul(o_att, w_branch_a, D_MODEL, epilogue="gate", gate=p_ga, name="branch_a")
    merged = matmul(o_rw, w_branch_b, D_MODEL, epilogue="gate", gate=p_gb, addend=m_a,
                    out_dtype=BF16, name="branch_b")
    x1 = matmul(merged, w_out, D_MODEL, epilogue="residual", addend=x, name="out_proj")

    rw_pad = jnp.pad(router_w.astype(F32), ((0, 0), (0, LANES - N_EXPERTS)))
    rb_pad = jnp.pad(router_b.astype(F32).reshape(1, -1), ((0, 0), (0, LANES - N_EXPERTS)),
                     constant_values=NEG)
    hn, gate, e_sel, rank, counts = router(x1, norm2_g, rw_pad, rb_pad, 256)
    tm = MOE_TM
    n_assign = n_tok * TOP_K
    n_blocks = -(-(n_assign + N_EXPERTS * (tm - 1)) // tm)
    counts = counts[0, :N_EXPERTS].astype(I32)
    padded = (counts + tm - 1) // tm * tm
    pad_end = jnp.cumsum(padded)
    pad_start = pad_end - padded
    dest = pad_start[e_sel[:, :TOP_K]] + rank[:, :TOP_K]
    row_tok = jnp.zeros((n_blocks * tm,), I32).at[dest.reshape(-1)].set(
        jnp.arange(n_assign, dtype=I32) // TOP_K)
    block_e = jnp.minimum(jnp.searchsorted(pad_end, jnp.arange(n_blocks, dtype=I32) * tm, side="right"),
                          N_EXPERTS - 1).astype(I32)
    n_used = (pad_end[-1:] // tm).astype(I32)
    xb = gather_rows(hn, row_tok, tm, BF16)
    act = moe_gate_up(xb, w_gu, b_gu, block_e, n_used, n_blocks)
    yb = moe_down(act, w_down, b_down, block_e, n_used, n_blocks)
    y_out = moe_combine(yb, x1, gate, dest, 128)

    lead = lambda a, lo, n, t: a[lo:lo + n * t].reshape(n, t, ATT_HEADS, ATT_HD)
    last_rw = lambda lo, n, t: p_rw[lo + t - 1:lo + n * t:t, :RW_COLS]
    return (y_out[:tok_p].reshape(n_p, t_p, d), y_out[tok_p:].reshape(n_s, t_s, d),
            lead(k_n, 0, n_p, t_p), lead(p_v, 0, n_p, t_p),
            lead(k_n, tok_p, n_s, t_s), lead(p_v, tok_p, n_s, t_s),
            _tiles_to_state(s_p), last_rw(0, n_p, t_p),
            _tiles_to_state(s_s), last_rw(tok_p, n_s, t_s))
```
